```python
import math
import jax
import jax.numpy as jnp
from jax import lax
import numpy as np

D_MODEL = 1024
BATCH = 32
SEQ = 2048
DEPTH = 1
DEC_BATCH = 32
DEC_SEQ = 16
PAST_LEN = 2048

CHUNK = 64
QBLOCK = 128
ROPE_THETA = 10000.0
EPS = 1e-6
NEG_INF = -1e30

H_A = 4
DK_A = 64
DV_A = 2 * DK_A
W_A = H_A * DV_A
H_B = 4
D_NOPE = 128
D_ROPE = 64
DV_B = 128
Q_LORA = 512
KV_LORA = 256
W_B = H_B * DV_B

COLS = (H_A * 2 * DK_A,
        H_A * 2 * DK_A,
        H_A * DV_A,
        W_A,
        Q_LORA,
        KV_LORA,
        D_ROPE,
        W_B,
        D_MODEL,
        D_MODEL)
SPLIT_POINTS = tuple(int(c) for c in np.cumsum(COLS)[:-1])
D_IN = int(sum(COLS))
DIFF_SCALE = DK_A ** -0.5
MLA_SCALE = (D_NOPE + D_ROPE) ** -0.5

kernel_name = 'diff_mla_gated_streaming_encoder'


def rms_norm(x, g):
    xf = x.astype(jnp.float32)
    y = xf * lax.rsqrt(jnp.mean(xf * xf, axis=-1, keepdims=True) + EPS)
    return (y * g.astype(jnp.float32)).astype(x.dtype)


def rope(x, pos):
    d = x.shape[-1]
    half = d // 2
    inv = ROPE_THETA ** (-jnp.arange(half, dtype=jnp.float32) * 2.0 / d)
    ang = pos.astype(jnp.float32)[:, None] * inv[None, :]
    shape = (pos.shape[0],) + (1,) * (x.ndim - 3) + (half,)
    cos = jnp.cos(ang).reshape(shape)
    sin = jnp.sin(ang).reshape(shape)
    xf = x.astype(jnp.float32)
    x1, x2 = xf[..., :half], xf[..., half:]
    return jnp.concatenate([x1 * cos - x2 * sin, x2 * cos + x1 * sin], axis=-1).astype(x.dtype)


def chunk_mask(q_pos, k_pos):
    return (k_pos[None, :] // CHUNK) <= (q_pos[:, None] // CHUNK)


def masked_softmax(scores, mask):
    return jax.nn.softmax(jnp.where(mask[None, None], scores, NEG_INF), axis=-1)


def diff_attention(q1, q2, q_pos, k1, k2, v, k_pos, lam):
    mask = chunk_mask(q_pos, k_pos)
    s1 = jnp.einsum('bqhd,bkhd->bhqk', q1, k1).astype(jnp.float32) * DIFF_SCALE
    s2 = jnp.einsum('bqhd,bkhd->bhqk', q2, k2).astype(jnp.float32) * DIFF_SCALE
    attn = masked_softmax(s1, mask) - lam * masked_softmax(s2, mask)
    return jnp.einsum('bhqk,bkhd->bqhd', attn.astype(v.dtype), v)


def mla_attention(q_lat, q_pe, q_pos, ckv, kpe, k_pos):
    mask = chunk_mask(q_pos, k_pos)
    s = (jnp.einsum('bqhc,bkc->bhqk', q_lat, ckv)
         + jnp.einsum('bqhr,bkr->bhqk', q_pe, kpe)).astype(jnp.float32) * MLA_SCALE
    p = masked_softmax(s, mask)
    return jnp.einsum('bhqk,bkc->bqhc', p.astype(ckv.dtype), ckv)


def sweep_query_blocks(fn, qs, q_pos):
    b, t = qs[0].shape[:2]
    nb = t // QBLOCK
    blocks = tuple(jnp.moveaxis(a.reshape((b, nb, QBLOCK) + a.shape[2:]), 1, 0) for a in qs)
    out = lax.map(lambda args: fn(*args[0], args[1]), (blocks, q_pos.reshape(nb, QBLOCK)))
    return jnp.moveaxis(out, 0, 1).reshape((b, t) + out.shape[3:])


def mixer_layer(x, pos, past, lam_init, w_in, w_uq, w_uk, w_uv, w_oa, w_ob, w_out,
                lq1, lk1, lq2, lk2, g_in, g_qa, g_kva, g_sub):
    b, t, _ = x.shape
    h = rms_norm(x, g_in)
    proj = jnp.einsum('btd,de->bte', h, w_in)
    qa, ka, va, z_a, qd, ckv_in, kpe_in, z_b, m_a, m_b = jnp.split(proj, SPLIT_POINTS, axis=-1)

    qa = rope(qa.reshape(b, t, H_A, 2, DK_A), pos)
    k_rows = rope(ka.reshape(b, t, H_A, 2, DK_A), pos).reshape(b, t, H_A, 2 * DK_A)
    v_rows = va.reshape(b, t, H_A, DV_A)

    q = jnp.einsum('btc,che->bthe', rms_norm(qd, g_qa), w_uq)
    q_lat = jnp.einsum('bthd,chd->bthc', q[..., :D_NOPE], w_uk)
    q_pe = rope(q[..., D_NOPE:], pos)
    ckv_rows = rms_norm(ckv_in, g_kva)
    kpe_rows = rope(kpe_in, pos)

    lam = (jnp.exp(jnp.sum(lq1.astype(jnp.float32) * lk1.astype(jnp.float32)))
           - jnp.exp(jnp.sum(lq2.astype(jnp.float32) * lk2.astype(jnp.float32))) + lam_init)

    if past is None:
        k_all, v_all, ckv_all, kpe_all, k_pos = k_rows, v_rows, ckv_rows, kpe_rows, pos
    else:
        ck, cv, cckv, ckpe = past
        p_len = ck.shape[1]
        k_all = jnp.concatenate([ck.astype(k_rows.dtype), k_rows], axis=1)
        v_all = jnp.concatenate([cv.astype(v_rows.dtype), v_rows], axis=1)
        ckv_all = jnp.concatenate([cckv.astype(ckv_rows.dtype), ckv_rows], axis=1)
        kpe_all = jnp.concatenate([ckpe.astype(kpe_rows.dtype), kpe_rows], axis=1)
        k_pos = jnp.concatenate([jnp.arange(p_len, dtype=jnp.int32), pos])
    kk = k_all.reshape(k_all.shape[:3] + (2, DK_A))
    k1, k2 = kk[..., 0, :], kk[..., 1, :]
    q1, q2 = qa[..., 0, :], qa[..., 1, :]

    diff_fn = lambda a1, a2, qp: diff_attention(a1, a2, qp, k1, k2, v_all, k_pos, lam)
    mla_fn = lambda a1, a2, qp: mla_attention(a1, a2, qp, ckv_all, kpe_all, k_pos)
    if past is None:
        o_a = sweep_query_blocks(diff_fn, (q1, q2), pos)
        o_lat = sweep_query_blocks(mla_fn, (q_lat, q_pe), pos)
    else:
        o_a = diff_fn(q1, q2, pos)
        o_lat = mla_fn(q_lat, q_pe, pos)

    o_a = (rms_norm(o_a, g_sub) * (1.0 - lam_init)).reshape(b, t, W_A) * jax.nn.silu(z_a)
    y_a = jnp.einsum('btw,wd->btd', o_a, w_oa)
    o_b = jnp.einsum('bthc,che->bthe', o_lat, w_uv).reshape(b, t, W_B) * jax.nn.silu(z_b)
    y_b = jnp.einsum('btw,wd->btd', o_b, w_ob)

    merged = jax.nn.sigmoid(m_a) * y_a + jax.nn.sigmoid(m_b) * y_b
    out = x + jnp.einsum('btd,de->bte', merged, w_out)
    return out, (k_rows, v_rows, ckv_rows, kpe_rows)


def setup_inputs(seed: int = 0) -> dict:
    key = jax.random.key(seed)
    ks = jax.random.split(key, 24)

    def nrm(k, shape, scale):
        return jax.random.normal(k, shape, jnp.float32) * scale

    def gain(k, shape):
        return 1.0 + 0.02 * jax.random.normal(k, shape, jnp.float32)

    return {
        'x_prompt': nrm(ks[0], (BATCH, SEQ, D_MODEL), 1.0),
        'x_sample': nrm(ks[1], (DEC_BATCH, DEC_SEQ, D_MODEL), 1.0),
        'cache_diff_k': nrm(ks[2], (DEPTH, DEC_BATCH, PAST_LEN, H_A, 2 * DK_A), 1.0),
        'cache_diff_v': nrm(ks[3], (DEPTH, DEC_BATCH, PAST_LEN, H_A, DV_A), 1.0),
        'cache_mla_ckv': nrm(ks[4], (DEPTH, DEC_BATCH, PAST_LEN, KV_LORA), 1.0),
        'cache_mla_kpe': nrm(ks[5], (DEPTH, DEC_BATCH, PAST_LEN, D_ROPE), 1.0),
        'w_in': nrm(ks[6], (DEPTH, D_MODEL, D_IN), D_MODEL ** -0.5),
        'w_uq': nrm(ks[7], (DEPTH, Q_LORA, H_B, D_NOPE + D_ROPE), Q_LORA ** -0.5),
        'w_uk': nrm(ks[8], (DEPTH, KV_LORA, H_B, D_NOPE), KV_LORA ** -0.5),
        'w_uv': nrm(ks[9], (DEPTH, KV_LORA, H_B, DV_B), KV_LORA ** -0.5),
        'w_oa': nrm(ks[10], (DEPTH, W_A, D_MODEL), W_A ** -0.5),
        'w_ob': nrm(ks[11], (DEPTH, W_B, D_MODEL), W_B ** -0.5),
        'w_out': nrm(ks[12], (DEPTH, D_MODEL, D_MODEL), D_MODEL ** -0.5),
        'lambda_q1': nrm(ks[13], (DEPTH, DK_A), 0.1),
        'lambda_k1': nrm(ks[14], (DEPTH, DK_A), 0.1),
        'lambda_q2': nrm(ks[15], (DEPTH, DK_A), 0.1),
        'lambda_k2': nrm(ks[16], (DEPTH, DK_A), 0.1),
        'norm_in': gain(ks[17], (DEPTH, D_MODEL)),
        'norm_qa': gain(ks[18], (DEPTH, Q_LORA)),
        'norm_kva': gain(ks[19], (DEPTH, KV_LORA)),
        'norm_subln': gain(ks[20], (DEPTH, DV_A)),
        'norm_final': gain(ks[21], (D_MODEL,)),
    }


def reference(x_prompt, x_sample, cache_diff_k, cache_diff_v, cache_mla_ckv, cache_mla_kpe,
              w_in, w_uq, w_uk, w_uv, w_oa, w_ob, w_out,
              lambda_q1, lambda_k1, lambda_q2, lambda_k2,
              norm_in, norm_qa, norm_kva, norm_subln, norm_final):
    pos_p = jnp.arange(x_prompt.shape[1], dtype=jnp.int32)
    past_len = cache_diff_k.shape[2]
    pos_s = past_len + jnp.arange(x_sample.shape[1], dtype=jnp.int32)

    hp, hs = x_prompt, x_sample
    st_p = ([], [], [], [])
    st_s = ([], [], [], [])
    for l in range(DEPTH):
        lam_init = 0.8 - 0.6 * math.exp(-0.3 * l)
        w = (w_in[l], w_uq[l], w_uk[l], w_uv[l], w_oa[l], w_ob[l], w_out[l],
             lambda_q1[l], lambda_k1[l], lambda_q2[l], lambda_k2[l],
             norm_in[l], norm_qa[l], norm_kva[l], norm_subln[l])
        hp, new_p = mixer_layer(hp, pos_p, None, lam_init, *w)
        past = (cache_diff_k[l], cache_diff_v[l], cache_mla_ckv[l], cache_mla_kpe[l])
        hs, new_s = mixer_layer(hs, pos_s, past, lam_init, *w)
        for lst, a in zip(st_p, new_p):
            lst.append(a)
        for lst, a in zip(st_s, new_s):
            lst.append(a)

    y_prompt = rms_norm(hp, norm_final)
    y_sample = rms_norm(hs, norm_final)
    new_diff_k_prompt = jnp.stack(st_p[0])
    new_diff_v_prompt = jnp.stack(st_p[1])
    new_ckv_prompt = jnp.stack(st_p[2])
    new_kpe_prompt = jnp.stack(st_p[3])
    new_diff_k_sample = jnp.stack(st_s[0])
    new_diff_v_sample = jnp.stack(st_s[1])
    new_ckv_sample = jnp.stack(st_s[2])
    new_kpe_sample = jnp.stack(st_s[3])
    return (y_prompt, y_sample, new_diff_k_prompt, new_diff_v_prompt, new_ckv_prompt, new_kpe_prompt,
            new_diff_k_sample, new_diff_v_sample, new_ckv_sample, new_kpe_sample)
```

```python
import functools
import math

import jax
import jax.numpy as jnp
from jax import lax
from jax.experimental import pallas as pl
from jax.experimental.pallas import tpu as pltpu

F32 = jnp.float32
BF16 = jnp.bfloat16

CHUNK = 64
ROPE_THETA = 10000.0
EPS = 1e-6
NEG_INF = -1e30

H_A = 4
DK_A = 64
DV_A = 2 * DK_A
W_A = H_A * DV_A
H_B = 4
D_NOPE = 128
D_ROPE = 64
DV_B = 128
Q_LORA = 512
KV_LORA = 256
W_B = H_B * DV_B
DIFF_SCALE = DK_A ** -0.5
MLA_SCALE = (D_NOPE + D_ROPE) ** -0.5

LANES = 128
KCAT = KV_LORA + LANES
VMEM_LIMIT = 56 * 1024 * 1024

C_QA = 0
C_KA = C_QA + 2 * H_A * DK_A
C_VA = C_KA + 2 * H_A * DK_A
C_ZA = C_VA + W_A
C_QD = C_ZA + W_A
C_CKV = C_QD + Q_LORA
C_KPE = C_CKV + KV_LORA
C_ZB = C_KPE + LANES
C_MA = C_ZB + W_B


def _rms(x, g):
    return x * lax.rsqrt(jnp.mean(x * x, axis=-1, keepdims=True) + EPS) * g


def _rope(x, cos, sin_signed):
    w = x.shape[-1]
    lane = lax.broadcasted_iota(jnp.int32, x.shape, 1)
    rot = jnp.where((lane % DK_A) < DK_A // 2, pltpu.roll(x, w - DK_A // 2, 1), pltpu.roll(x, DK_A // 2, 1))
    return x * cos + rot * sin_signed


def _dot(a, b):
    return jnp.dot(a, b, preferred_element_type=F32)


def _dot_nt(a, b):
    return lax.dot_general(a, b, (((1,), (1,)), ((), ())), preferred_element_type=F32)


def _in_proj_kernel(d_model, x_ref, cos_ref, sin_ref, w_in_ref, g_in_ref, g_qa_ref, g_kva_ref,
                    w_uqn_ref, w_uqr_ref, w_ukt_ref,
                    q_ref, k_ref, v_ref, za_ref, zb_ref, ma_ref, mb_ref, ckv_ref, kpe_ref, kcat_ref, qcat_ref):
    h = _rms(x_ref[...], g_in_ref[...]).astype(BF16)
    cos = cos_ref[...]
    sin = sin_ref[...]

    def proj(c0, n):
        return _dot(h, w_in_ref[:, c0:c0 + n])

    p = proj(C_QA, 2 * H_A * DK_A)
    for j in range(H_A):
        sl = slice(j * LANES, (j + 1) * LANES)
        q_ref[:, sl] = (_rope(p[:, sl], cos, sin) * DIFF_SCALE).astype(BF16)
    p = proj(C_KA, 2 * H_A * DK_A)
    for j in range(H_A):
        sl = slice(j * LANES, (j + 1) * LANES)
        k_ref[:, sl] = _rope(p[:, sl], cos, sin)
    v_ref[...] = proj(C_VA, W_A)
    za_ref[...] = proj(C_ZA, W_A).astype(BF16)

    qd = _rms(proj(C_QD, Q_LORA), g_qa_ref[...]).astype(BF16)
    qn = _dot(qd, w_uqn_ref[...]).astype(BF16)
    qr = _dot(qd, w_uqr_ref[...])
    for j in range(H_B):
        sl = slice(j * LANES, (j + 1) * LANES)
        q_lat = _dot(qn[:, sl], w_ukt_ref[j])
        qcat_ref[j, :, 0:KV_LORA] = (q_lat * MLA_SCALE).astype(BF16)
        qcat_ref[j, :, KV_LORA:KCAT] = (_rope(qr[:, sl], cos, sin) * MLA_SCALE).astype(BF16)

    ckv = _rms(proj(C_CKV, KV_LORA), g_kva_ref[...])
    ckv_ref[...] = ckv
    kcat_ref[:, 0:KV_LORA] = ckv.astype(BF16)
    kp = _rope(proj(C_KPE, LANES), cos, sin)
    kpe_ref[...] = kp[:, 0:D_ROPE]
    kcat_ref[:, KV_LORA:KCAT] = kp.astype(BF16)

    zb_ref[...] = proj(C_ZB, W_B).astype(BF16)
    ma_ref[...] = proj(C_MA, d_model).astype(BF16)
    mb_ref[...] = proj(C_MA + d_model, d_model).astype(BF16)


def _in_proj(x, cos, sin, n_pos_blocks, tm, wts):
    n, d = x.shape
    d_in = wts["w_in"].shape[1]
    const = lambda shape: pl.BlockSpec(shape, lambda i: (0,) * len(shape))
    row = lambda w: pl.BlockSpec((tm, w), lambda i: (i, 0))
    pos = pl.BlockSpec((tm, LANES), lambda i: (i % n_pos_blocks, 0))
    out_shape = (
        jax.ShapeDtypeStruct((n, 2 * H_A * DK_A), BF16),
        jax.ShapeDtypeStruct((n, 2 * H_A * DK_A), F32),
        jax.ShapeDtypeStruct((n, W_A), F32),
        jax.ShapeDtypeStruct((n, W_A), BF16),
        jax.ShapeDtypeStruct((n, W_B), BF16),
        jax.ShapeDtypeStruct((n, d), BF16),
        jax.ShapeDtypeStruct((n, d), BF16),
        jax.ShapeDtypeStruct((n, KV_LORA), F32),
        jax.ShapeDtypeStruct((n, D_ROPE), F32),
        jax.ShapeDtypeStruct((n, KCAT), BF16),
        jax.ShapeDtypeStruct((H_B, n, KCAT), BF16),
    )
    out_specs = (row(512), row(512), row(W_A), row(W_A), row(W_B), row(d), row(d), row(KV_LORA), row(D_ROPE),
                 row(KCAT), pl.BlockSpec((H_B, tm, KCAT), lambda i: (0, i, 0)))
    return pl.pallas_call(
        functools.partial(_in_proj_kernel, d),
        grid=(n // tm,),
        in_specs=[row(d), pos, pos, const((d, d_in)), const((1, d)), const((1, Q_LORA)), const((1, KV_LORA)),
                  const((Q_LORA, H_B * D_NOPE)), const((Q_LORA, H_B * LANES)), const((H_B, D_NOPE, KV_LORA))],
        out_specs=out_specs,
        out_shape=out_shape,
        compiler_params=pltpu.CompilerParams(dimension_semantics=("arbitrary",), vmem_limit_bytes=VMEM_LIMIT),
        name="in_proj",
    )(x, cos, sin, wts["w_in"], wts["g_in"], wts["g_qa"], wts["g_kva"], wts["w_uqn"], wts["w_uqr"], wts["w_ukt"])


def _lambda(lam_ref, lam_init):
    lp = lam_ref[...]
    a1 = jnp.sum(lp[0:1] * lp[1:2], axis=1, keepdims=True)
    a2 = jnp.sum(lp[2:3] * lp[3:4], axis=1, keepdims=True)
    return jnp.exp(a1) - jnp.exp(a2) + lam_init


def _softmax_step(qs, kblk, vblk, carry, mask):
    m, l, acc = carry
    s = _dot_nt(qs, kblk)
    if mask is not None:
        s = jnp.where(mask, s, NEG_INF)
    m_new = jnp.maximum(m, jnp.max(s, axis=1, keepdims=True))
    p = jnp.exp(s - m_new)
    alpha = jnp.exp(m - m_new)
    l = alpha * l + jnp.sum(p, axis=1, keepdims=True)
    acc = alpha * acc + _dot(p.astype(BF16), vblk)
    return m_new, l, acc


def _diag_mask(rows, tq, tk):
    r = lax.broadcasted_iota(jnp.int32, (rows, tk), 0) % tq
    c = lax.broadcasted_iota(jnp.int32, (rows, tk), 1)
    return (c // CHUNK) <= (r // CHUNK)


def _attend(qs, k_sc, v_of, qi, tq, dv):
    rows = qs.shape[0]

    def full_block(j, carry):
        kblk = k_sc[pl.ds(pl.multiple_of(j * tq, tq), tq), :]
        return _softmax_step(qs, kblk, v_of(kblk, j), carry, None)

    carry = (jnp.full((rows, 1), NEG_INF, F32), jnp.zeros((rows, 1), F32), jnp.zeros((rows, dv), F32))
    carry = lax.fori_loop(0, qi, full_block, carry)
    kblk = k_sc[pl.ds(pl.multiple_of(qi * tq, tq), tq), :]
    _, l, acc = _softmax_step(qs, kblk, v_of(kblk, qi), carry, _diag_mask(rows, tq, tq))
    return acc / l


def _diff_attn_kernel(lam_init, tq, lam_ref, gsub_ref, q_ref, k_ref, v_ref, o_ref, kb, vb):
    qi = pl.program_id(2)

    @pl.when(qi == 0)
    def _():
        kb[...] = k_ref[0].astype(BF16)
        vb[...] = v_ref[0].astype(BF16)

    q = q_ref[0]
    lane = lax.broadcasted_iota(jnp.int32, q.shape, 1)
    zero = jnp.zeros_like(q)
    qs = jnp.concatenate([jnp.where(lane < DK_A, q, zero), jnp.where(lane >= DK_A, q, zero)], axis=0)

    def v_of(_, j):
        return vb[pl.ds(pl.multiple_of(j * tq, tq), tq), :]

    o = _attend(qs, kb, v_of, qi, tq, DV_A)
    od = o[:tq] - _lambda(lam_ref, lam_init) * o[tq:]
    o_ref[0] = (_rms(od, gsub_ref[...]) * (1.0 - lam_init)).astype(BF16)


def _diff_attn(q, k, v, lam_p, g_sub, lam_init, tq):
    b, t, _ = q.shape
    blk_q = pl.BlockSpec((1, tq, LANES), lambda bi, h, qi: (bi, qi, h))
    blk_kv = pl.BlockSpec((1, t, LANES), lambda bi, h, qi: (bi, 0, h))
    return pl.pallas_call(
        functools.partial(_diff_attn_kernel, lam_init, tq),
        grid=(b, H_A, t // tq),
        in_specs=[pl.BlockSpec((4, DK_A), lambda bi, h, qi: (0, 0)),
                  pl.BlockSpec((1, DV_A), lambda bi, h, qi: (0, 0)), blk_q, blk_kv, blk_kv],
        out_specs=blk_q,
        out_shape=jax.ShapeDtypeStruct((b, t, W_A), BF16),
        scratch_shapes=[pltpu.VMEM((t, LANES), BF16), pltpu.VMEM((t, LANES), BF16)],
        compiler_params=pltpu.CompilerParams(dimension_semantics=("arbitrary", "arbitrary", "arbitrary"),
                                             vmem_limit_bytes=VMEM_LIMIT),
        name="diff_attn",
    )(lam_p, g_sub, q, k, v)


def _mla_attn_kernel(tq, q_ref, kc_ref, w_uvt_ref, o_ref):
    qi = pl.program_id(1)
    qs = q_ref[:, 0].reshape(H_B * tq, KCAT)

    def v_of(kblk, _):
        return kblk[:, 0:KV_LORA]

    o_lat = _attend(qs, kc_ref.at[0], v_of, qi, tq, KV_LORA).astype(BF16)
    for h in range(H_B):
        o_ref[0, :, h * DV_B:(h + 1) * DV_B] = _dot(o_lat[h * tq:(h + 1) * tq], w_uvt_ref[h]).astype(BF16)


def _mla_attn(qcat, kcat, w_uvt, tq):
    _, b, t, _ = qcat.shape
    return pl.pallas_call(
        functools.partial(_mla_attn_kernel, tq),
        grid=(b, t // tq),
        in_specs=[pl.BlockSpec((H_B, 1, tq, KCAT), lambda bi, qi: (0, bi, qi, 0)),
                  pl.BlockSpec((1, t, KCAT), lambda bi, qi: (bi, 0, 0)),
                  pl.BlockSpec((H_B, KV_LORA, DV_B), lambda bi, qi: (0, 0, 0))],
        out_specs=pl.BlockSpec((1, tq, W_B), lambda bi, qi: (bi, qi, 0)),
        out_shape=jax.ShapeDtypeStruct((b, t, W_B), BF16),
        compiler_params=pltpu.CompilerParams(dimension_semantics=("arbitrary", "arbitrary"),
                                             vmem_limit_bytes=VMEM_LIMIT),
        name="mla_attn",
    )(qcat, kcat, w_uvt)


def _two_part_softmax(s_past, s_new, v_past, v_new):
    m = jnp.maximum(jnp.max(s_past, axis=1, keepdims=True), jnp.max(s_new, axis=1, keepdims=True))
    p_past = jnp.exp(s_past - m)
    p_new = jnp.exp(s_new - m)
    l = jnp.sum(p_past, axis=1, keepdims=True) + jnp.sum(p_new, axis=1, keepdims=True)
    return (_dot(p_past.astype(BF16), v_past) + _dot(p_new.astype(BF16), v_new)) / l


def _sample_attn_kernel(lam_init, past_len, ts, lam_ref, gsub_ref, w_uvt_ref, q_ref, ck_ref, cv_ref, kn_ref, vn_ref,
                        qcat_ref, cckv_ref, ckpe_ref, kcn_ref, oa_ref, ob_ref):
    def new_mask(rows):
        r = past_len + lax.broadcasted_iota(jnp.int32, (rows, ts), 0) % ts
        c = past_len + lax.broadcasted_iota(jnp.int32, (rows, ts), 1)
        return (c // CHUNK) <= (r // CHUNK)

    lam = _lambda(lam_ref, lam_init)
    mask2 = new_mask(2 * ts)
    for h in range(H_A):
        sl = slice(h * LANES, (h + 1) * LANES)
        q = q_ref[0, :, sl]
        lane = lax.broadcasted_iota(jnp.int32, q.shape, 1)
        zero = jnp.zeros_like(q)
        qs = jnp.concatenate([jnp.where(lane < DK_A, q, zero), jnp.where(lane >= DK_A, q, zero)], axis=0)
        k_new = kn_ref[0, :, sl].astype(BF16)
        s_past = _dot_nt(qs, ck_ref[0, :, sl].astype(BF16))
        s_new = jnp.where(mask2, _dot_nt(qs, k_new), NEG_INF)
        o = _two_part_softmax(s_past, s_new, cv_ref[0, :, sl].astype(BF16), vn_ref[0, :, sl].astype(BF16))
        od = o[:ts] - lam * o[ts:]
        oa_ref[0, :, sl] = (_rms(od, gsub_ref[...]) * (1.0 - lam_init)).astype(BF16)

    qs = qcat_ref[:, 0].reshape(H_B * ts, KCAT)
    ckv = cckv_ref[0].astype(BF16)
    kc_new = kcn_ref[0]
    s_past = _dot_nt(qs[:, 0:KV_LORA], ckv) + _dot_nt(qs[:, KV_LORA:KV_LORA + D_ROPE], ckpe_ref[0].astype(BF16))
    s_new = jnp.where(new_mask(H_B * ts), _dot_nt(qs, kc_new), NEG_INF)
    o_lat = _two_part_softmax(s_past, s_new, ckv, kc_new[:, 0:KV_LORA]).astype(BF16)
    for h in range(H_B):
        ob_ref[0, :, h * DV_B:(h + 1) * DV_B] = _dot(o_lat[h * ts:(h + 1) * ts], w_uvt_ref[h]).astype(BF16)


def _sample_attn(q, ck, cv, kn, vn, qcat, cckv, ckpe, kcn, lam_p, g_sub, w_uvt, lam_init):
    b, ts, _ = q.shape
    past_len = ck.shape[1]
    per_b = lambda rows, w: pl.BlockSpec((1, rows, w), lambda bi: (bi, 0, 0))
    const = lambda shape: pl.BlockSpec(shape, lambda bi: (0,) * len(shape))
    return pl.pallas_call(
        functools.partial(_sample_attn_kernel, lam_init, past_len, ts),
        grid=(b,),
        in_specs=[const((4, DK_A)), const((1, DV_A)), const((H_B, KV_LORA, DV_B)),
                  per_b(ts, W_A), per_b(past_len, W_A), per_b(past_len, W_A), per_b(ts, W_A), per_b(ts, W_A),
                  pl.BlockSpec((H_B, 1, ts, KCAT), lambda bi: (0, bi, 0, 0)),
                  per_b(past_len, KV_LORA), per_b(past_len, D_ROPE), per_b(ts, KCAT)],
        out_specs=(per_b(ts, W_A), per_b(ts, W_B)),
        out_shape=(jax.ShapeDtypeStruct((b, ts, W_A), BF16), jax.ShapeDtypeStruct((b, ts, W_B), BF16)),
        compiler_params=pltpu.CompilerParams(dimension_semantics=("arbitrary",), vmem_limit_bytes=VMEM_LIMIT),
        name="sample_attn",
    )(lam_p, g_sub, w_uvt, q, ck, cv, kn, vn, qcat, cckv, ckpe, kcn)


def _out_proj_kernel(final_norm, x_ref, oa_ref, ob_ref, za_ref, zb_ref, ma_ref, mb_ref,
                     w_oa_ref, w_ob_ref, w_out_ref, g_fin_ref, y_ref):
    a = (oa_ref[...].astype(F32) * jax.nn.silu(za_ref[...].astype(F32))).astype(BF16)
    y_a = _dot(a, w_oa_ref[...])
    bb = (ob_ref[...].astype(F32) * jax.nn.silu(zb_ref[...].astype(F32))).astype(BF16)
    y_b = _dot(bb, w_ob_ref[...])
    merged = jax.nn.sigmoid(ma_ref[...].astype(F32)) * y_a + jax.nn.sigmoid(mb_ref[...].astype(F32)) * y_b
    out = x_ref[...] + _dot(merged.astype(BF16), w_out_ref[...])
    y_ref[...] = _rms(out, g_fin_ref[...]) if final_norm else out


def _out_proj(x, oa, ob, za, zb, ma, mb, wts, g_final, final_norm, tm):
    n, d = x.shape
    const = lambda shape: pl.BlockSpec(shape, lambda i: (0,) * len(shape))
    row = lambda w: pl.BlockSpec((tm, w), lambda i: (i, 0))
    return pl.pallas_call(
        functools.partial(_out_proj_kernel, final_norm),
        grid=(n // tm,),
        in_specs=[row(d), row(W_A), row(W_B), row(W_A), row(W_B), row(d), row(d),
                  const((W_A, d)), const((W_B, d)), const((d, d)), const((1, d))],
        out_specs=row(d),
        out_shape=jax.ShapeDtypeStruct((n, d), F32),
        compiler_params=pltpu.CompilerParams(dimension_semantics=("arbitrary",), vmem_limit_bytes=VMEM_LIMIT),
        name="out_proj",
    )(x, oa, ob, za, zb, ma, mb, wts["w_oa"], wts["w_ob"], wts["w_out"], g_final)


def _rope_tables(pos):
    half = DK_A // 2
    inv = ROPE_THETA ** (-jnp.arange(half, dtype=F32) * 2.0 / DK_A)
    ang = pos.astype(F32)[:, None] * inv[None, :]
    cos, sin = jnp.cos(ang), jnp.sin(ang)
    cos = jnp.concatenate([cos, cos, cos, cos], axis=-1)
    sin = jnp.concatenate([-sin, sin, -sin, sin], axis=-1)
    return cos, sin


def _layer_weights(l, w_in, w_uq, w_uk, w_uv, w_oa, w_ob, w_out, norm_in, norm_qa, norm_kva):
    d = w_in.shape[1]
    c_kpe = C_KPE
    wi = w_in[l]
    wi = jnp.concatenate([wi[:, :c_kpe + D_ROPE], jnp.zeros((d, LANES - D_ROPE), wi.dtype), wi[:, c_kpe + D_ROPE:]], axis=1)
    uq = w_uq[l]
    uqr = jnp.pad(uq[:, :, D_NOPE:], ((0, 0), (0, 0), (0, LANES - D_ROPE)))
    return {
        "w_in": wi.astype(BF16),
        "w_uqn": uq[:, :, :D_NOPE].reshape(Q_LORA, H_B * D_NOPE).astype(BF16),
        "w_uqr": uqr.reshape(Q_LORA, H_B * LANES).astype(BF16),
        "w_ukt": jnp.transpose(w_uk[l], (1, 2, 0)).astype(BF16),
        "w_uvt": jnp.transpose(w_uv[l], (1, 0, 2)).astype(BF16),
        "w_oa": w_oa[l].astype(BF16),
        "w_ob": w_ob[l].astype(BF16),
        "w_out": w_out[l].astype(BF16),
        "g_in": norm_in[l][None, :],
        "g_qa": norm_qa[l][None, :],
        "g_kva": norm_kva[l][None, :],
    }


def _row_tile(n, cap):
    tm = min(n, cap)
    assert n % tm == 0
    return tm


def kernel(x_prompt, x_sample, cache_diff_k, cache_diff_v, cache_mla_ckv, cache_mla_kpe, w_in, w_uq, w_uk, w_uv,
           w_oa, w_ob, w_out, lambda_q1, lambda_k1, lambda_q2, lambda_k2, norm_in, norm_qa, norm_kva, norm_subln,
           norm_final):
    b, t, d = x_prompt.shape
    bs, ts, _ = x_sample.shape
    depth = w_in.shape[0]
    past_len = cache_diff_k.shape[2]
    tq = _row_tile(t, 256)
    tm_p = _row_tile(t, 512)
    tm_s = _row_tile(bs * ts, 512)
    assert tm_s % ts == 0 and tq % CHUNK == 0

    cos_p, sin_p = _rope_tables(jnp.arange(t, dtype=jnp.int32))
    cos_s, sin_s = _rope_tables(past_len + jnp.arange(ts, dtype=jnp.int32))
    cos_s = jnp.tile(cos_s, (tm_s // ts, 1))
    sin_s = jnp.tile(sin_s, (tm_s // ts, 1))
    g_final = norm_final[None, :]

    hp = x_prompt.reshape(b * t, d)
    hs = x_sample.reshape(bs * ts, d)
    new_p = ([], [], [], [])
    new_s = ([], [], [], [])
    for l in range(depth):
        lam_init = 0.8 - 0.6 * math.exp(-0.3 * l)
        last = l == depth - 1
        wts = _layer_weights(l, w_in, w_uq, w_uk, w_uv, w_oa, w_ob, w_out, norm_in, norm_qa, norm_kva)
        lam_p = jnp.stack([lambda_q1[l], lambda_k1[l], lambda_q2[l], lambda_k2[l]])
        g_sub = norm_subln[l][None, :]

        q, k, v, za, zb, ma, mb, ckv, kpe, kcat, qcat = _in_proj(hp, cos_p, sin_p, t // tm_p, tm_p, wts)
        oa = _diff_attn(q.reshape(b, t, W_A), k.reshape(b, t, W_A), v.reshape(b, t, W_A), lam_p, g_sub, lam_init, tq)
        ob = _mla_attn(qcat.reshape(H_B, b, t, KCAT), kcat.reshape(b, t, KCAT), wts["w_uvt"], tq)
        hp = _out_proj(hp, oa.reshape(b * t, W_A), ob.reshape(b * t, W_B), za, zb, ma, mb, wts, g_final, last, tm_p)
        for lst, a in zip(new_p, (k.reshape(b, t, H_A, 2 * DK_A), v.reshape(b, t, H_A, DV_A),
                                  ckv.reshape(b, t, KV_LORA), kpe.reshape(b, t, D_ROPE))):
            lst.append(a)

        q, k, v, za, zb, ma, mb, ckv, kpe, kcat, qcat = _in_proj(hs, cos_s, sin_s, 1, tm_s, wts)
        oa, ob = _sample_attn(
            q.reshape(bs, ts, W_A), cache_diff_k[l].reshape(bs, past_len, W_A),
            cache_diff_v[l].reshape(bs, past_len, W_A), k.reshape(bs, ts, W_A), v.reshape(bs, ts, W_A),
            qcat.reshape(H_B, bs, ts, KCAT), cache_mla_ckv[l], cache_mla_kpe[l], kcat.reshape(bs, ts, KCAT),
            lam_p, g_sub, wts["w_uvt"], lam_init)
        hs = _out_proj(hs, oa.reshape(bs * ts, W_A), ob.reshape(bs * ts, W_B), za, zb, ma, mb, wts, g_final, last, tm_s)
        for lst, a in zip(new_s, (k.reshape(bs, ts, H_A, 2 * DK_A), v.reshape(bs, ts, H_A, DV_A),
                                  ckv.reshape(bs, ts, KV_LORA), kpe.reshape(bs, ts, D_ROPE))):
            lst.append(a)

    return (hp.reshape(b, t, d), hs.reshape(bs, ts, d),
            jnp.stack(new_p[0]), jnp.stack(new_p[1]), jnp.stack(new_p[2]), jnp.stack(new_p[3]),
            jnp.stack(new_s[0]), jnp.stack(new_s[1]), jnp.stack(new_s[2]), jnp.stack(new_s[3]))
```

```python
import functools
import math

import jax
import jax.numpy as jnp
from jax import lax
from jax.experimental import pallas as pl
from jax.experimental.pallas import tpu as pltpu

F32 = jnp.float32
BF16 = jnp.bfloat16

CHUNK = 64
ROPE_THETA = 10000.0
EPS = 1e-6
NEG_INF = -1e30
LOG2E = math.log2(math.e)

H_A = 4
DK_A = 64
DV_A = 2 * DK_A
W_A = H_A * DV_A
H_B = 4
D_NOPE = 128
D_ROPE = 64
DV_B = 128
Q_LORA = 512
KV_LORA = 256
W_B = H_B * DV_B
DIFF_SCALE = DK_A ** -0.5 * LOG2E
MLA_SCALE = (D_NOPE + D_ROPE) ** -0.5 * LOG2E

LANES = 128
KCAT = KV_LORA + LANES
VMEM_LIMIT = 56 * 1024 * 1024

C_QA = 0
C_KA = C_QA + 2 * H_A * DK_A
C_VA = C_KA + 2 * H_A * DK_A
C_ZA = C_VA + W_A
C_QD = C_ZA + W_A
C_CKV = C_QD + Q_LORA
C_KPE = C_CKV + KV_LORA
C_ZB = C_KPE + LANES
C_MA = C_ZB + W_B


def _rms(x, g):
    return x * lax.rsqrt(jnp.mean(x * x, axis=-1, keepdims=True) + EPS) * g


def _rope(x, cos, sin_signed):
    w = x.shape[-1]
    lane = lax.broadcasted_iota(jnp.int32, x.shape, 1)
    rot = jnp.where((lane % DK_A) < DK_A // 2, pltpu.roll(x, w - DK_A // 2, 1), pltpu.roll(x, DK_A // 2, 1))
    return x * cos + rot * sin_signed


def _dot(a, b):
    return jnp.dot(a, b, preferred_element_type=F32)


def _dot_nt(a, b):
    return lax.dot_general(a, b, (((1,), (1,)), ((), ())), preferred_element_type=F32)


def _in_proj_kernel(d_model, x_ref, cos_ref, sin_ref, w_in_ref, g_in_ref, g_qa_ref, g_kva_ref,
                    w_uqn_ref, w_uqr_ref, w_ukt_ref,
                    q_ref, k_ref, v_ref, kb_ref, vb_ref, za_ref, zb_ref, ma_ref, mb_ref, ckv_ref, kpe_ref,
                    kcat_ref, qcat_ref):
    tm = x_ref.shape[0]
    h = _rms(x_ref[...], g_in_ref[...]).astype(BF16)
    cos = cos_ref[...]
    sin = sin_ref[...]

    def proj(c0, n):
        return _dot(h, w_in_ref[:, c0:c0 + n])

    p = proj(C_QA, 2 * H_A * DK_A)
    for j in range(H_A):
        sl = slice(j * LANES, (j + 1) * LANES)
        q_ref[:, sl] = (_rope(p[:, sl], cos, sin) * DIFF_SCALE).astype(BF16)
    p = proj(C_KA, 2 * H_A * DK_A)
    for j in range(H_A):
        sl = slice(j * LANES, (j + 1) * LANES)
        kr = _rope(p[:, sl], cos, sin)
        k_ref[pl.ds(j, tm, stride=H_A), :] = kr
        kb_ref[:, sl] = kr.astype(BF16)
    p = proj(C_VA, W_A)
    for j in range(H_A):
        v_ref[pl.ds(j, tm, stride=H_A), :] = p[:, j * LANES:(j + 1) * LANES]
    vb_ref[...] = p.astype(BF16)
    za_ref[...] = proj(C_ZA, W_A).astype(BF16)

    qd = _rms(proj(C_QD, Q_LORA), g_qa_ref[...]).astype(BF16)
    qn = _dot(qd, w_uqn_ref[...]).astype(BF16)
    qr = _dot(qd, w_uqr_ref[...])
    for j in range(H_B):
        sl = slice(j * LANES, (j + 1) * LANES)
        q_lat = _dot(qn[:, sl], w_ukt_ref[j])
        qcat_ref[j, :, 0:KV_LORA] = (q_lat * MLA_SCALE).astype(BF16)
        qcat_ref[j, :, KV_LORA:KCAT] = (_rope(qr[:, sl], cos, sin) * MLA_SCALE).astype(BF16)

    ckv = _rms(proj(C_CKV, KV_LORA), g_kva_ref[...])
    ckv_ref[...] = ckv
    kcat_ref[:, 0:KV_LORA] = ckv.astype(BF16)
    kp = _rope(proj(C_KPE, LANES), cos, sin)
    kpe_ref[...] = kp[:, 0:D_ROPE]
    kcat_ref[:, KV_LORA:KCAT] = kp.astype(BF16)

    zb_ref[...] = proj(C_ZB, W_B).astype(BF16)
    ma_ref[...] = proj(C_MA, d_model).astype(BF16)
    mb_ref[...] = proj(C_MA + d_model, d_model).astype(BF16)


def _in_proj(x, cos, sin, n_pos_blocks, tm, wts):
    n, d = x.shape
    d_in = wts["w_in"].shape[1]
    const = lambda shape: pl.BlockSpec(shape, lambda i: (0,) * len(shape))
    row = lambda w: pl.BlockSpec((tm, w), lambda i: (i, 0))
    heads = pl.BlockSpec((tm * H_A, LANES), lambda i: (i, 0))
    pos = pl.BlockSpec((tm, LANES), lambda i: (i % n_pos_blocks, 0))
    out_shape = (
        jax.ShapeDtypeStruct((n, W_A), BF16),
        jax.ShapeDtypeStruct((n * H_A, 2 * DK_A), F32),
        jax.ShapeDtypeStruct((n * H_A, DV_A), F32),
        jax.ShapeDtypeStruct((n, W_A), BF16),
        jax.ShapeDtypeStruct((n, W_A), BF16),
        jax.ShapeDtypeStruct((n, W_A), BF16),
        jax.ShapeDtypeStruct((n, W_B), BF16),
        jax.ShapeDtypeStruct((n, d), BF16),
        jax.ShapeDtypeStruct((n, d), BF16),
        jax.ShapeDtypeStruct((n, KV_LORA), F32),
        jax.ShapeDtypeStruct((n, D_ROPE), F32),
        jax.ShapeDtypeStruct((n, KCAT), BF16),
        jax.ShapeDtypeStruct((H_B, n, KCAT), BF16),
    )
    out_specs = (row(W_A), heads, heads, row(W_A), row(W_A), row(W_A), row(W_B), row(d), row(d), row(KV_LORA),
                 row(D_ROPE), row(KCAT), pl.BlockSpec((H_B, tm, KCAT), lambda i: (0, i, 0)))
    return pl.pallas_call(
        functools.partial(_in_proj_kernel, d),
        grid=(n // tm,),
        in_specs=[row(d), pos, pos, const((d, d_in)), const((1, d)), const((1, Q_LORA)), const((1, KV_LORA)),
                  const((Q_LORA, H_B * D_NOPE)), const((Q_LORA, H_B * LANES)), const((H_B, D_NOPE, KV_LORA))],
        out_specs=out_specs,
        out_shape=out_shape,
        compiler_params=pltpu.CompilerParams(dimension_semantics=("arbitrary",), vmem_limit_bytes=VMEM_LIMIT),
        name="in_proj",
    )(x, cos, sin, wts["w_in"], wts["g_in"], wts["g_qa"], wts["g_kva"], wts["w_uqn"], wts["w_uqr"], wts["w_ukt"])


def _lambda(lam_ref, lam_init):
    lp = lam_ref[...]
    a1 = jnp.sum(lp[0:1] * lp[1:2], axis=1, keepdims=True)
    a2 = jnp.sum(lp[2:3] * lp[3:4], axis=1, keepdims=True)
    return jnp.exp(a1) - jnp.exp(a2) + lam_init


def _stack_diff_queries(q):
    lane = lax.broadcasted_iota(jnp.int32, q.shape, 1)
    zero = jnp.zeros_like(q)
    return jnp.concatenate([jnp.where(lane < DK_A, q, zero), jnp.where(lane >= DK_A, q, zero)], axis=0)


def _softmax_step(qs, kblk, vblk, carry, mask):
    m, l, acc = carry
    s = _dot_nt(qs, kblk)
    if mask is not None:
        s = jnp.where(mask, s, NEG_INF)
    m_new = jnp.maximum(m, jnp.max(s, axis=1, keepdims=True))
    p = jnp.exp2(s - m_new)
    alpha = jnp.exp2(m - m_new)
    l = alpha * l + jnp.sum(p, axis=1, keepdims=True)
    acc = alpha * acc + _dot(p.astype(BF16), vblk)
    return m_new, l, acc


def _softmax_init(rows, dv):
    return jnp.full((rows, 1), NEG_INF, F32), jnp.zeros((rows, 1), F32), jnp.zeros((rows, dv), F32)


def _diag_mask(rows, tq):
    r = lax.broadcasted_iota(jnp.int32, (rows, tq), 0) % tq
    c = lax.broadcasted_iota(jnp.int32, (rows, tq), 1)
    return (c // CHUNK) <= (r // CHUNK)


def _diff_attn_kernel(lam_init, tq, lam_ref, gsub_ref, q_ref, k_ref, v_ref, o_ref):
    qi = pl.program_id(1)
    q = q_ref[0]
    qs = [_stack_diff_queries(q[:, h * LANES:(h + 1) * LANES]) for h in range(H_A)]

    def block(j, carry, mask):
        rows = pl.ds(pl.multiple_of(j * tq, tq), tq)
        return tuple(
            _softmax_step(qs[h], k_ref[0, rows, h * LANES:(h + 1) * LANES], v_ref[0, rows, h * LANES:(h + 1) * LANES],
                          carry[h], mask)
            for h in range(H_A))

    carry = tuple(_softmax_init(2 * tq, DV_A) for _ in range(H_A))
    carry = lax.fori_loop(0, qi, lambda j, c: block(j, c, None), carry)
    carry = block(qi, carry, _diag_mask(2 * tq, tq))
    lam = _lambda(lam_ref, lam_init)
    for h in range(H_A):
        _, l, acc = carry[h]
        o = acc / l
        od = o[:tq] - lam * o[tq:]
        o_ref[0, :, h * LANES:(h + 1) * LANES] = (_rms(od, gsub_ref[...]) * (1.0 - lam_init)).astype(BF16)


def _diff_attn(q, k, v, lam_p, g_sub, lam_init, tq):
    b, t, _ = q.shape
    blk_q = pl.BlockSpec((1, tq, W_A), lambda bi, qi: (bi, qi, 0))
    blk_kv = pl.BlockSpec((1, t, W_A), lambda bi, qi: (bi, 0, 0))
    return pl.pallas_call(
        functools.partial(_diff_attn_kernel, lam_init, tq),
        grid=(b, t // tq),
        in_specs=[pl.BlockSpec((4, DK_A), lambda bi, qi: (0, 0)), pl.BlockSpec((1, DV_A), lambda bi, qi: (0, 0)),
                  blk_q, blk_kv, blk_kv],
        out_specs=blk_q,
        out_shape=jax.ShapeDtypeStruct((b, t, W_A), BF16),
        compiler_params=pltpu.CompilerParams(dimension_semantics=("arbitrary", "arbitrary"),
                                             vmem_limit_bytes=VMEM_LIMIT),
        name="diff_attn",
    )(lam_p, g_sub, q, k, v)


def _mla_attn_kernel(tq, q_ref, kc_ref, w_uvt_ref, o_ref):
    qi = pl.program_id(1)
    qs = q_ref[:, 0].reshape(H_B * tq, KCAT)

    def block(j, carry, mask):
        kblk = kc_ref[0, pl.ds(pl.multiple_of(j * tq, tq), tq), :]
        return _softmax_step(qs, kblk, kblk[:, 0:KV_LORA], carry, mask)

    carry = lax.fori_loop(0, qi, lambda j, c: block(j, c, None), _softmax_init(H_B * tq, KV_LORA))
    _, l, acc = block(qi, carry, _diag_mask(H_B * tq, tq))
    o_lat = (acc / l).astype(BF16)
    for h in range(H_B):
        o_ref[0, :, h * DV_B:(h + 1) * DV_B] = _dot(o_lat[h * tq:(h + 1) * tq], w_uvt_ref[h]).astype(BF16)


def _mla_attn(qcat, kcat, w_uvt, tq):
    _, b, t, _ = qcat.shape
    return pl.pallas_call(
        functools.partial(_mla_attn_kernel, tq),
        grid=(b, t // tq),
        in_specs=[pl.BlockSpec((H_B, 1, tq, KCAT), lambda bi, qi: (0, bi, qi, 0)),
                  pl.BlockSpec((1, t, KCAT), lambda bi, qi: (bi, 0, 0)),
                  pl.BlockSpec((H_B, KV_LORA, DV_B), lambda bi, qi: (0, 0, 0))],
        out_specs=pl.BlockSpec((1, tq, W_B), lambda bi, qi: (bi, qi, 0)),
        out_shape=jax.ShapeDtypeStruct((b, t, W_B), BF16),
        compiler_params=pltpu.CompilerParams(dimension_semantics=("arbitrary", "arbitrary"),
                                             vmem_limit_bytes=VMEM_LIMIT),
        name="mla_attn",
    )(qcat, kcat, w_uvt)


def _two_part_softmax(s_past, s_new, v_past, v_new):
    m = jnp.maximum(jnp.max(s_past, axis=1, keepdims=True), jnp.max(s_new, axis=1, keepdims=True))
    p_past = jnp.exp2(s_past - m)
    p_new = jnp.exp2(s_new - m)
    l = jnp.sum(p_past, axis=1, keepdims=True) + jnp.sum(p_new, axis=1, keepdims=True)
    return (_dot(p_past.astype(BF16), v_past) + _dot(p_new.astype(BF16), v_new)) / l


def _sample_attn_kernel(lam_init, past_len, ts, lam_ref, gsub_ref, w_uvt_ref, q_ref, ck_ref, cv_ref, kn_ref, vn_ref,
                        qcat_ref, cckv_ref, ckpe_t_ref, kcn_ref, oa_ref, ob_ref):
    def new_mask(rows):
        r = past_len + lax.broadcasted_iota(jnp.int32, (rows, ts), 0) % ts
        c = past_len + lax.broadcasted_iota(jnp.int32, (rows, ts), 1)
        return (c // CHUNK) <= (r // CHUNK)

    lam = _lambda(lam_ref, lam_init)
    mask2 = new_mask(2 * ts)
    for h in range(H_A):
        sl = slice(h * LANES, (h + 1) * LANES)
        qs = _stack_diff_queries(q_ref[0, :, sl])
        head_rows = pl.ds(h, past_len, stride=H_A)
        s_past = _dot_nt(qs, ck_ref[0, head_rows, :].astype(BF16))
        s_new = jnp.where(mask2, _dot_nt(qs, kn_ref[0, :, sl]), NEG_INF)
        o = _two_part_softmax(s_past, s_new, cv_ref[0, head_rows, :].astype(BF16), vn_ref[0, :, sl])
        od = o[:ts] - lam * o[ts:]
        oa_ref[0, :, sl] = (_rms(od, gsub_ref[...]) * (1.0 - lam_init)).astype(BF16)

    qs = qcat_ref[:, 0].reshape(H_B * ts, KCAT)
    ckv = cckv_ref[0].astype(BF16)
    kc_new = kcn_ref[0]
    s_past = (_dot_nt(qs[:, 0:KV_LORA], ckv)
              + _dot(qs[:, KV_LORA:KV_LORA + D_ROPE], ckpe_t_ref[0].astype(BF16)))
    s_new = jnp.where(new_mask(H_B * ts), _dot_nt(qs, kc_new), NEG_INF)
    o_lat = _two_part_softmax(s_past, s_new, ckv, kc_new[:, 0:KV_LORA]).astype(BF16)
    for h in range(H_B):
        ob_ref[0, :, h * DV_B:(h + 1) * DV_B] = _dot(o_lat[h * ts:(h + 1) * ts], w_uvt_ref[h]).astype(BF16)


def _sample_attn(q, ck, cv, kn, vn, qcat, cckv, ckpe_t, kcn, lam_p, g_sub, w_uvt, lam_init):
    b, ts, _ = q.shape
    past_len = cckv.shape[1]
    per_b = lambda rows, w: pl.BlockSpec((1, rows, w), lambda bi: (bi, 0, 0))
    const = lambda shape: pl.BlockSpec(shape, lambda bi: (0,) * len(shape))
    return pl.pallas_call(
        functools.partial(_sample_attn_kernel, lam_init, past_len, ts),
        grid=(b,),
        in_specs=[const((4, DK_A)), const((1, DV_A)), const((H_B, KV_LORA, DV_B)),
                  per_b(ts, W_A), per_b(past_len * H_A, LANES), per_b(past_len * H_A, LANES),
                  per_b(ts, W_A), per_b(ts, W_A),
                  pl.BlockSpec((H_B, 1, ts, KCAT), lambda bi: (0, bi, 0, 0)),
                  per_b(past_len, KV_LORA), per_b(D_ROPE, past_len), per_b(ts, KCAT)],
        out_specs=(per_b(ts, W_A), per_b(ts, W_B)),
        out_shape=(jax.ShapeDtypeStruct((b, ts, W_A), BF16), jax.ShapeDtypeStruct((b, ts, W_B), BF16)),
        compiler_params=pltpu.CompilerParams(dimension_semantics=("arbitrary",), vmem_limit_bytes=VMEM_LIMIT),
        name="sample_attn",
    )(lam_p, g_sub, w_uvt, q, ck, cv, kn, vn, qcat, cckv, ckpe_t, kcn)


def _out_proj_kernel(final_norm, x_ref, oa_ref, ob_ref, za_ref, zb_ref, ma_ref, mb_ref,
                     w_oa_ref, w_ob_ref, w_out_ref, g_fin_ref, y_ref):
    a = (oa_ref[...].astype(F32) * jax.nn.silu(za_ref[...].astype(F32))).astype(BF16)
    y_a = _dot(a, w_oa_ref[...])
    bb = (ob_ref[...].astype(F32) * jax.nn.silu(zb_ref[...].astype(F32))).astype(BF16)
    y_b = _dot(bb, w_ob_ref[...])
    merged = jax.nn.sigmoid(ma_ref[...].astype(F32)) * y_a + jax.nn.sigmoid(mb_ref[...].astype(F32)) * y_b
    out = x_ref[...] + _dot(merged.astype(BF16), w_out_ref[...])
    y_ref[...] = _rms(out, g_fin_ref[...]) if final_norm else out


def _out_proj(x, oa, ob, za, zb, ma, mb, wts, g_final, final_norm, tm):
    n, d = x.shape
    const = lambda shape: pl.BlockSpec(shape, lambda i: (0,) * len(shape))
    row = lambda w: pl.BlockSpec((tm, w), lambda i: (i, 0))
    return pl.pallas_call(
        functools.partial(_out_proj_kernel, final_norm),
        grid=(n // tm,),
        in_specs=[row(d), row(W_A), row(W_B), row(W_A), row(W_B), row(d), row(d),
                  const((W_A, d)), const((W_B, d)), const((d, d)), const((1, d))],
        out_specs=row(d),
        out_shape=jax.ShapeDtypeStruct((n, d), F32),
        compiler_params=pltpu.CompilerParams(dimension_semantics=("arbitrary",), vmem_limit_bytes=VMEM_LIMIT),
        name="out_proj",
    )(x, oa, ob, za, zb, ma, mb, wts["w_oa"], wts["w_ob"], wts["w_out"], g_final)


def _rope_tables(pos):
    half = DK_A // 2
    inv = ROPE_THETA ** (-jnp.arange(half, dtype=F32) * 2.0 / DK_A)
    ang = pos.astype(F32)[:, None] * inv[None, :]
    cos, sin = jnp.cos(ang), jnp.sin(ang)
    cos = jnp.concatenate([cos, cos, cos, cos], axis=-1)
    sin = jnp.concatenate([-sin, sin, -sin, sin], axis=-1)
    return cos, sin


def _layer_weights(l, w_in, w_uq, w_uk, w_uv, w_oa, w_ob, w_out, norm_in, norm_qa, norm_kva):
    d = w_in.shape[1]
    wi = w_in[l]
    wi = jnp.concatenate([wi[:, :C_KPE + D_ROPE], jnp.zeros((d, LANES - D_ROPE), wi.dtype), wi[:, C_KPE + D_ROPE:]],
                         axis=1)
    uq = w_uq[l]
    uqr = jnp.pad(uq[:, :, D_NOPE:], ((0, 0), (0, 0), (0, LANES - D_ROPE)))
    return {
        "w_in": wi.astype(BF16),
        "w_uqn": uq[:, :, :D_NOPE].reshape(Q_LORA, H_B * D_NOPE).astype(BF16),
        "w_uqr": uqr.reshape(Q_LORA, H_B * LANES).astype(BF16),
        "w_ukt": jnp.transpose(w_uk[l], (1, 2, 0)).astype(BF16),
        "w_uvt": jnp.transpose(w_uv[l], (1, 0, 2)).astype(BF16),
        "w_oa": w_oa[l].astype(BF16),
        "w_ob": w_ob[l].astype(BF16),
        "w_out": w_out[l].astype(BF16),
        "g_in": norm_in[l][None, :],
        "g_qa": norm_qa[l][None, :],
        "g_kva": norm_kva[l][None, :],
    }


def _row_tile(n, cap):
    tm = min(n, cap)
    assert n % tm == 0
    return tm


def kernel(x_prompt, x_sample, cache_diff_k, cache_diff_v, cache_mla_ckv, cache_mla_kpe, w_in, w_uq, w_uk, w_uv,
           w_oa, w_ob, w_out, lambda_q1, lambda_k1, lambda_q2, lambda_k2, norm_in, norm_qa, norm_kva, norm_subln,
           norm_final):
    b, t, d = x_prompt.shape
    bs, ts, _ = x_sample.shape
    depth = w_in.shape[0]
    past_len = cache_diff_k.shape[2]
    tq = _row_tile(t, 512)
    tm_p = _row_tile(t, 512)
    tm_s = _row_tile(bs * ts, 512)
    assert tm_s % ts == 0 and tq % CHUNK == 0

    cos_p, sin_p = _rope_tables(jnp.arange(t, dtype=jnp.int32))
    cos_s, sin_s = _rope_tables(past_len + jnp.arange(ts, dtype=jnp.int32))
    cos_s = jnp.tile(cos_s, (tm_s // ts, 1))
    sin_s = jnp.tile(sin_s, (tm_s // ts, 1))
    g_final = norm_final[None, :]

    hp = x_prompt.reshape(b * t, d)
    hs = x_sample.reshape(bs * ts, d)
    new_p = ([], [], [], [])
    new_s = ([], [], [], [])
    for l in range(depth):
        lam_init = 0.8 - 0.6 * math.exp(-0.3 * l)
        last = l == depth - 1
        wts = _layer_weights(l, w_in, w_uq, w_uk, w_uv, w_oa, w_ob, w_out, norm_in, norm_qa, norm_kva)
        lam_p = jnp.stack([lambda_q1[l], lambda_k1[l], lambda_q2[l], lambda_k2[l]])
        g_sub = norm_subln[l][None, :]

        q, k, v, kb, vb, za, zb, ma, mb, ckv, kpe, kcat, qcat = _in_proj(hp, cos_p, sin_p, t // tm_p, tm_p, wts)
        oa = _diff_attn(q.reshape(b, t, W_A), kb.reshape(b, t, W_A), vb.reshape(b, t, W_A), lam_p, g_sub, lam_init, tq)
        ob = _mla_attn(qcat.reshape(H_B, b, t, KCAT), kcat.reshape(b, t, KCAT), wts["w_uvt"], tq)
        hp = _out_proj(hp, oa.reshape(b * t, W_A), ob.reshape(b * t, W_B), za, zb, ma, mb, wts, g_final, last, tm_p)
        for lst, a in zip(new_p, (k.reshape(b, t, H_A, 2 * DK_A), v.reshape(b, t, H_A, DV_A),
                                  ckv.reshape(b, t, KV_LORA), kpe.reshape(b, t, D_ROPE))):
            lst.append(a)

        q, k, v, kb, vb, za, zb, ma, mb, ckv, kpe, kcat, qcat = _in_proj(hs, cos_s, sin_s, 1, tm_s, wts)
        oa, ob = _sample_attn(
            q.reshape(bs, ts, W_A), cache_diff_k[l].reshape(bs, past_len * H_A, 2 * DK_A),
            cache_diff_v[l].reshape(bs, past_len * H_A, DV_A), kb.reshape(bs, ts, W_A), vb.reshape(bs, ts, W_A),
            qcat.reshape(H_B, bs, ts, KCAT), cache_mla_ckv[l], jnp.swapaxes(cache_mla_kpe[l], 1, 2),
            kcat.reshape(bs, ts, KCAT), lam_p, g_sub, wts["w_uvt"], lam_init)
        hs = _out_proj(hs, oa.reshape(bs * ts, W_A), ob.reshape(bs * ts, W_B), za, zb, ma, mb, wts, g_final, last, tm_s)
        for lst, a in zip(new_s, (k.reshape(bs, ts, H_A, 2 * DK_A), v.reshape(bs, ts, H_A, DV_A),
                                  ckv.reshape(bs, ts, KV_LORA), kpe.reshape(bs, ts, D_ROPE))):
            lst.append(a)

    return (hp.reshape(b, t, d), hs.reshape(bs, ts, d),
            jnp.stack(new_p[0]), jnp.stack(new_p[1]), jnp.stack(new_p[2]), jnp.stack(new_p[3]),
            jnp.stack(new_s[0]), jnp.stack(new_s[1]), jnp.stack(new_s[2]), jnp.stack(new_s[3]))
```

```python
import functools
import math

import jax
import jax.numpy as jnp
from jax import lax
from jax.experimental import pallas as pl
from jax.experimental.pallas import tpu as pltpu

F32 = jnp.float32
BF16 = jnp.bfloat16

CHUNK = 64
ROPE_THETA = 10000.0
EPS = 1e-6
NEG_INF = -1e30
LOG2E = math.log2(math.e)

H_A = 4
DK_A = 64
DV_A = 2 * DK_A
W_A = H_A * DV_A
H_B = 4
D_NOPE = 128
D_ROPE = 64
DV_B = 128
Q_LORA = 512
KV_LORA = 256
W_B = H_B * DV_B
DIFF_SCALE = DK_A ** -0.5 * LOG2E
MLA_SCALE = (D_NOPE + D_ROPE) ** -0.5 * LOG2E

LANES = 128
KCAT = KV_LORA + LANES
VMEM_LIMIT = 56 * 1024 * 1024

C_QA = 0
C_KA = C_QA + 2 * H_A * DK_A
C_VA = C_KA + 2 * H_A * DK_A
C_ZA = C_VA + W_A
C_QD = C_ZA + W_A
C_CKV = C_QD + Q_LORA
C_KPE = C_CKV + KV_LORA
C_ZB = C_KPE + LANES
C_MA = C_ZB + W_B


def _rms(x, g):
    return x * lax.rsqrt(jnp.mean(x * x, axis=-1, keepdims=True) + EPS) * g


def _rope(x, cos, sin_signed):
    w = x.shape[-1]
    lane = lax.broadcasted_iota(jnp.int32, x.shape, 1)
    rot = jnp.where((lane % DK_A) < DK_A // 2, pltpu.roll(x, w - DK_A // 2, 1), pltpu.roll(x, DK_A // 2, 1))
    return x * cos + rot * sin_signed


def _dot(a, b):
    return jnp.dot(a, b, preferred_element_type=F32)


def _dot_nt(a, b):
    return lax.dot_general(a, b, (((1,), (1,)), ((), ())), preferred_element_type=F32)


def _in_proj_kernel(d_model, x_ref, cos_ref, sin_ref, w_in_ref, g_in_ref, g_qa_ref, g_kva_ref,
                    w_uqn_ref, w_uqr_ref, w_ukt_ref,
                    q_ref, qt_ref, k_ref, v_ref, kb_ref, vb_ref, vt_ref, za_ref, zb_ref, ma_ref, mb_ref, ckv_ref,
                    kpe_ref, kcat_ref, qcat_ref):
    tm = x_ref.shape[0]
    h = _rms(x_ref[...], g_in_ref[...]).astype(BF16)
    cos = cos_ref[...]
    sin = sin_ref[...]

    def proj(c0, n):
        return _dot(h, w_in_ref[:, c0:c0 + n])

    p = proj(C_QA, 2 * H_A * DK_A)
    for j in range(H_A):
        sl = slice(j * LANES, (j + 1) * LANES)
        qr = _rope(p[:, sl], cos, sin) * DIFF_SCALE
        q_ref[:, sl] = qr.astype(BF16)
        qt_ref[0, sl, :] = qr.T.astype(BF16)
    p = proj(C_KA, 2 * H_A * DK_A)
    for j in range(H_A):
        sl = slice(j * LANES, (j + 1) * LANES)
        kr = _rope(p[:, sl], cos, sin)
        k_ref[pl.ds(j, tm, stride=H_A), :] = kr
        kb_ref[:, sl] = kr.astype(BF16)
    p = proj(C_VA, W_A)
    for j in range(H_A):
        sl = slice(j * LANES, (j + 1) * LANES)
        v_ref[pl.ds(j, tm, stride=H_A), :] = p[:, sl]
        vt_ref[0, sl, :] = p[:, sl].T.astype(BF16)
    vb_ref[...] = p.astype(BF16)
    za_ref[...] = proj(C_ZA, W_A).astype(BF16)

    qd = _rms(proj(C_QD, Q_LORA), g_qa_ref[...]).astype(BF16)
    qn = _dot(qd, w_uqn_ref[...]).astype(BF16)
    qr = _dot(qd, w_uqr_ref[...])
    for j in range(H_B):
        sl = slice(j * LANES, (j + 1) * LANES)
        q_lat = _dot(qn[:, sl], w_ukt_ref[j])
        qcat_ref[j, :, 0:KV_LORA] = (q_lat * MLA_SCALE).astype(BF16)
        qcat_ref[j, :, KV_LORA:KCAT] = (_rope(qr[:, sl], cos, sin) * MLA_SCALE).astype(BF16)

    ckv = _rms(proj(C_CKV, KV_LORA), g_kva_ref[...])
    ckv_ref[...] = ckv
    kcat_ref[:, 0:KV_LORA] = ckv.astype(BF16)
    kp = _rope(proj(C_KPE, LANES), cos, sin)
    kpe_ref[...] = kp[:, 0:D_ROPE]
    kcat_ref[:, KV_LORA:KCAT] = kp.astype(BF16)

    zb_ref[...] = proj(C_ZB, W_B).astype(BF16)
    ma_ref[...] = proj(C_MA, d_model).astype(BF16)
    mb_ref[...] = proj(C_MA + d_model, d_model).astype(BF16)


def _in_proj(x, cos, sin, n_pos_blocks, tm, wts):
    n, d = x.shape
    d_in = wts["w_in"].shape[1]
    const = lambda shape: pl.BlockSpec(shape, lambda i: (0,) * len(shape))
    row = lambda w: pl.BlockSpec((tm, w), lambda i: (i, 0))
    heads = pl.BlockSpec((tm * H_A, LANES), lambda i: (i, 0))
    pos = pl.BlockSpec((tm, LANES), lambda i: (i % n_pos_blocks, 0))
    out_shape = (
        jax.ShapeDtypeStruct((n, W_A), BF16),
        jax.ShapeDtypeStruct((n // tm, W_A, tm), BF16),
        jax.ShapeDtypeStruct((n * H_A, 2 * DK_A), F32),
        jax.ShapeDtypeStruct((n * H_A, DV_A), F32),
        jax.ShapeDtypeStruct((n, W_A), BF16),
        jax.ShapeDtypeStruct((n, W_A), BF16),
        jax.ShapeDtypeStruct((n // tm, W_A, tm), BF16),
        jax.ShapeDtypeStruct((n, W_A), BF16),
        jax.ShapeDtypeStruct((n, W_B), BF16),
        jax.ShapeDtypeStruct((n, d), BF16),
        jax.ShapeDtypeStruct((n, d), BF16),
        jax.ShapeDtypeStruct((n, KV_LORA), F32),
        jax.ShapeDtypeStruct((n, D_ROPE), F32),
        jax.ShapeDtypeStruct((n, KCAT), BF16),
        jax.ShapeDtypeStruct((H_B, n, KCAT), BF16),
    )
    tile_t = pl.BlockSpec((1, W_A, tm), lambda i: (i, 0, 0))
    out_specs = (row(W_A), tile_t, heads, heads, row(W_A), row(W_A), tile_t, row(W_A), row(W_B), row(d), row(d),
                 row(KV_LORA), row(D_ROPE), row(KCAT), pl.BlockSpec((H_B, tm, KCAT), lambda i: (0, i, 0)))
    return pl.pallas_call(
        functools.partial(_in_proj_kernel, d),
        grid=(n // tm,),
        in_specs=[row(d), pos, pos, const((d, d_in)), const((1, d)), const((1, Q_LORA)), const((1, KV_LORA)),
                  const((Q_LORA, H_B * D_NOPE)), const((Q_LORA, H_B * LANES)), const((H_B, D_NOPE, KV_LORA))],
        out_specs=out_specs,
        out_shape=out_shape,
        compiler_params=pltpu.CompilerParams(dimension_semantics=("arbitrary",), vmem_limit_bytes=VMEM_LIMIT),
        name="in_proj",
    )(x, cos, sin, wts["w_in"], wts["g_in"], wts["g_qa"], wts["g_kva"], wts["w_uqn"], wts["w_uqr"], wts["w_ukt"])


def _lambda(lam_ref, lam_init):
    lp = lam_ref[...]
    a1 = jnp.sum(lp[0:1] * lp[1:2], axis=1, keepdims=True)
    a2 = jnp.sum(lp[2:3] * lp[3:4], axis=1, keepdims=True)
    return jnp.exp(a1) - jnp.exp(a2) + lam_init


def _stack_diff_queries(q):
    lane = lax.broadcasted_iota(jnp.int32, q.shape, 1)
    zero = jnp.zeros_like(q)
    return jnp.concatenate([jnp.where(lane < DK_A, q, zero), jnp.where(lane >= DK_A, q, zero)], axis=0)


def _softmax_step(qs, kblk, vblk, carry, mask):
    m, l, acc = carry
    s = _dot_nt(qs, kblk)
    if mask is not None:
        s = jnp.where(mask, s, NEG_INF)
    m_new = jnp.maximum(m, jnp.max(s, axis=1, keepdims=True))
    p = jnp.exp2(s - m_new)
    alpha = jnp.exp2(m - m_new)
    l = alpha * l + jnp.sum(p, axis=1, keepdims=True)
    acc = alpha * acc + _dot(p.astype(BF16), vblk)
    return m_new, l, acc


def _softmax_init(rows, dv):
    return jnp.full((rows, 1), NEG_INF, F32), jnp.zeros((rows, 1), F32), jnp.zeros((rows, dv), F32)


def _diag_mask(rows, tq):
    r = lax.broadcasted_iota(jnp.int32, (rows, tq), 0) % tq
    c = lax.broadcasted_iota(jnp.int32, (rows, tq), 1)
    return (c // CHUNK) <= (r // CHUNK)


def _diag_mask_t(tq, cols):
    r = lax.broadcasted_iota(jnp.int32, (tq, cols), 0)
    c = lax.broadcasted_iota(jnp.int32, (tq, cols), 1) % tq
    return (r // CHUNK) <= (c // CHUNK)


def _diff_attn_kernel(lam_init, tq, nq, lam_ref, gsub_ref, qt_ref, k_ref, vt_ref, o_ref):
    qi = pl.program_id(1)
    lam = _lambda(lam_ref, lam_init)
    mask = _diag_mask_t(tq, 2 * tq)

    def branch(nb):
        def score_piece(h, j):
            sl = slice(h * LANES, (h + 1) * LANES)
            qt = qt_ref[0, 0, sl, :]
            row = lax.broadcasted_iota(jnp.int32, qt.shape, 0)
            zero = jnp.zeros_like(qt)
            qst = jnp.concatenate([jnp.where(row < DK_A, qt, zero), jnp.where(row >= DK_A, qt, zero)], axis=1)
            sj = _dot(k_ref[0, j * tq:(j + 1) * tq, sl], qst)
            return jnp.where(mask, sj, NEG_INF) if j == nb - 1 else sj

        s_next = [score_piece(0, j) for j in range(nb)]
        for h in range(H_A):
            sl = slice(h * LANES, (h + 1) * LANES)
            s_cur, s_next = s_next, []
            pending = [(h + 1, j) for j in range(nb)] if h + 1 < H_A else []

            def pump(k):
                while pending and len(s_next) * 3 < k:
                    s_next.append(score_piece(*pending.pop(0)))

            m = None
            for j in range(nb):
                mj = jnp.max(s_cur[j], axis=0, keepdims=True)
                m = mj if m is None else jnp.maximum(m, mj)
                pump(j + 1)
            es, l = [], None
            for j in range(nb):
                e = jnp.exp2(s_cur[j] - m)
                es.append(e)
                lj = jnp.sum(e, axis=0, keepdims=True)
                l = lj if l is None else l + lj
                pump(nb + j + 1)
            l1 = l[:, :tq]
            r = lam * l1 / l[:, tq:]
            o = None
            for j in range(nb):
                pj = (es[j][:, :tq] - r * es[j][:, tq:]).astype(BF16)
                oj = _dot(vt_ref[0, j, sl, :], pj)
                o = oj if o is None else o + oj
                pump(2 * nb + j + 1)
            pump(4 * nb)
            od = o / l1
            y = od * lax.rsqrt(jnp.mean(od * od, axis=0, keepdims=True) + EPS) * gsub_ref[...] * (1.0 - lam_init)
            o_ref[0, :, sl] = y.T.astype(BF16)

    for blk in range(nq):
        pl.when(qi == blk)(functools.partial(branch, blk + 1))


def _diff_attn(qt, k, vt, lam_p, g_sub_col, lam_init, tq):
    b, t, _ = k.shape
    nq = t // tq
    return pl.pallas_call(
        functools.partial(_diff_attn_kernel, lam_init, tq, nq),
        grid=(b, nq),
        in_specs=[pl.BlockSpec((4, DK_A), lambda bi, qi: (0, 0)), pl.BlockSpec((DV_A, 1), lambda bi, qi: (0, 0)),
                  pl.BlockSpec((1, 1, W_A, tq), lambda bi, qi: (bi, qi, 0, 0)),
                  pl.BlockSpec((1, t, W_A), lambda bi, qi: (bi, 0, 0)),
                  pl.BlockSpec((1, nq, W_A, tq), lambda bi, qi: (bi, 0, 0, 0))],
        out_specs=pl.BlockSpec((1, tq, W_A), lambda bi, qi: (bi, qi, 0)),
        out_shape=jax.ShapeDtypeStruct((b, t, W_A), BF16),
        compiler_params=pltpu.CompilerParams(dimension_semantics=("arbitrary", "arbitrary"),
                                             vmem_limit_bytes=VMEM_LIMIT),
        name="diff_attn",
    )(lam_p, g_sub_col, qt, k, vt)


def _mla_attn_kernel(tq, q_ref, kc_ref, w_uvt_ref, o_ref):
    qi = pl.program_id(1)
    qs = q_ref[:, 0].reshape(H_B * tq, KCAT)

    def block(j, carry, mask):
        kblk = kc_ref[0, pl.ds(pl.multiple_of(j * tq, tq), tq), :]
        return _softmax_step(qs, kblk, kblk[:, 0:KV_LORA], carry, mask)

    carry = lax.fori_loop(0, qi, lambda j, c: block(j, c, None), _softmax_init(H_B * tq, KV_LORA))
    _, l, acc = block(qi, carry, _diag_mask(H_B * tq, tq))
    o_lat = (acc / l).astype(BF16)
    for h in range(H_B):
        o_ref[0, :, h * DV_B:(h + 1) * DV_B] = _dot(o_lat[h * tq:(h + 1) * tq], w_uvt_ref[h]).astype(BF16)


def _mla_attn(qcat, kcat, w_uvt, tq):
    _, b, t, _ = qcat.shape
    return pl.pallas_call(
        functools.partial(_mla_attn_kernel, tq),
        grid=(b, t // tq),
        in_specs=[pl.BlockSpec((H_B, 1, tq, KCAT), lambda bi, qi: (0, bi, qi, 0)),
                  pl.BlockSpec((1, t, KCAT), lambda bi, qi: (bi, 0, 0)),
                  pl.BlockSpec((H_B, KV_LORA, DV_B), lambda bi, qi: (0, 0, 0))],
        out_specs=pl.BlockSpec((1, tq, W_B), lambda bi, qi: (bi, qi, 0)),
        out_shape=jax.ShapeDtypeStruct((b, t, W_B), BF16),
        compiler_params=pltpu.CompilerParams(dimension_semantics=("arbitrary", "arbitrary"),
                                             vmem_limit_bytes=VMEM_LIMIT),
        name="mla_attn",
    )(qcat, kcat, w_uvt)


def _two_part_softmax(s_past, s_new, v_past, v_new):
    m = jnp.maximum(jnp.max(s_past, axis=1, keepdims=True), jnp.max(s_new, axis=1, keepdims=True))
    p_past = jnp.exp2(s_past - m)
    p_new = jnp.exp2(s_new - m)
    l = jnp.sum(p_past, axis=1, keepdims=True) + jnp.sum(p_new, axis=1, keepdims=True)
    return (_dot(p_past.astype(BF16), v_past) + _dot(p_new.astype(BF16), v_new)) / l


def _sample_attn_kernel(lam_init, past_len, ts, lam_ref, gsub_ref, w_uvt_ref, q_ref, ck_ref, cv_ref, kn_ref, vn_ref,
                        qcat_ref, cckv_ref, ckpe_t_ref, kcn_ref, oa_ref, ob_ref):
    def new_mask(rows):
        r = past_len + lax.broadcasted_iota(jnp.int32, (rows, ts), 0) % ts
        c = past_len + lax.broadcasted_iota(jnp.int32, (rows, ts), 1)
        return (c // CHUNK) <= (r // CHUNK)

    lam = _lambda(lam_ref, lam_init)
    mask2 = new_mask(2 * ts)
    for h in range(H_A):
        sl = slice(h * LANES, (h + 1) * LANES)
        qs = _stack_diff_queries(q_ref[0, :, sl])
        head_rows = pl.ds(h, past_len, stride=H_A)
        s_past = _dot_nt(qs, ck_ref[0, head_rows, :].astype(BF16))
        s_new = jnp.where(mask2, _dot_nt(qs, kn_ref[0, :, sl]), NEG_INF)
        o = _two_part_softmax(s_past, s_new, cv_ref[0, head_rows, :].astype(BF16), vn_ref[0, :, sl])
        od = o[:ts] - lam * o[ts:]
        oa_ref[0, :, sl] = (_rms(od, gsub_ref[...]) * (1.0 - lam_init)).astype(BF16)

    qs = qcat_ref[:, 0].reshape(H_B * ts, KCAT)
    ckv = cckv_ref[0].astype(BF16)
    kc_new = kcn_ref[0]
    s_past = (_dot_nt(qs[:, 0:KV_LORA], ckv)
              + _dot(qs[:, KV_LORA:KV_LORA + D_ROPE], ckpe_t_ref[0].astype(BF16)))
    s_new = jnp.where(new_mask(H_B * ts), _dot_nt(qs, kc_new), NEG_INF)
    o_lat = _two_part_softmax(s_past, s_new, ckv, kc_new[:, 0:KV_LORA]).astype(BF16)
    for h in range(H_B):
        ob_ref[0, :, h * DV_B:(h + 1) * DV_B] = _dot(o_lat[h * ts:(h + 1) * ts], w_uvt_ref[h]).astype(BF16)


def _sample_attn(q, ck, cv, kn, vn, qcat, cckv, ckpe_t, kcn, lam_p, g_sub, w_uvt, lam_init):
    b, ts, _ = q.shape
    past_len = cckv.shape[1]
    per_b = lambda rows, w: pl.BlockSpec((1, rows, w), lambda bi: (bi, 0, 0))
    const = lambda shape: pl.BlockSpec(shape, lambda bi: (0,) * len(shape))
    return pl.pallas_call(
        functools.partial(_sample_attn_kernel, lam_init, past_len, ts),
        grid=(b,),
        in_specs=[const((4, DK_A)), const((1, DV_A)), const((H_B, KV_LORA, DV_B)),
                  per_b(ts, W_A), per_b(past_len * H_A, LANES), per_b(past_len * H_A, LANES),
                  per_b(ts, W_A), per_b(ts, W_A),
                  pl.BlockSpec((H_B, 1, ts, KCAT), lambda bi: (0, bi, 0, 0)),
                  per_b(past_len, KV_LORA), per_b(D_ROPE, past_len), per_b(ts, KCAT)],
        out_specs=(per_b(ts, W_A), per_b(ts, W_B)),
        out_shape=(jax.ShapeDtypeStruct((b, ts, W_A), BF16), jax.ShapeDtypeStruct((b, ts, W_B), BF16)),
        compiler_params=pltpu.CompilerParams(dimension_semantics=("arbitrary",), vmem_limit_bytes=VMEM_LIMIT),
        name="sample_attn",
    )(lam_p, g_sub, w_uvt, q, ck, cv, kn, vn, qcat, cckv, ckpe_t, kcn)


def _out_proj_kernel(final_norm, x_ref, oa_ref, ob_ref, za_ref, zb_ref, ma_ref, mb_ref,
                     w_oa_ref, w_ob_ref, w_out_ref, g_fin_ref, y_ref):
    a = (oa_ref[...].astype(F32) * jax.nn.silu(za_ref[...].astype(F32))).astype(BF16)
    y_a = _dot(a, w_oa_ref[...])
    bb = (ob_ref[...].astype(F32) * jax.nn.silu(zb_ref[...].astype(F32))).astype(BF16)
    y_b = _dot(bb, w_ob_ref[...])
    merged = jax.nn.sigmoid(ma_ref[...].astype(F32)) * y_a + jax.nn.sigmoid(mb_ref[...].astype(F32)) * y_b
    out = x_ref[...] + _dot(merged.astype(BF16), w_out_ref[...])
    y_ref[...] = _rms(out, g_fin_ref[...]) if final_norm else out


def _out_proj(x, oa, ob, za, zb, ma, mb, wts, g_final, final_norm, tm):
    n, d = x.shape
    const = lambda shape: pl.BlockSpec(shape, lambda i: (0,) * len(shape))
    row = lambda w: pl.BlockSpec((tm, w), lambda i: (i, 0))
    return pl.pallas_call(
        functools.partial(_out_proj_kernel, final_norm),
        grid=(n // tm,),
        in_specs=[row(d), row(W_A), row(W_B), row(W_A), row(W_B), row(d), row(d),
                  const((W_A, d)), const((W_B, d)), const((d, d)), const((1, d))],
        out_specs=row(d),
        out_shape=jax.ShapeDtypeStruct((n, d), F32),
        compiler_params=pltpu.CompilerParams(dimension_semantics=("arbitrary",), vmem_limit_bytes=VMEM_LIMIT),
        name="out_proj",
    )(x, oa, ob, za, zb, ma, mb, wts["w_oa"], wts["w_ob"], wts["w_out"], g_final)


def _rope_tables(pos):
    half = DK_A // 2
    inv = ROPE_THETA ** (-jnp.arange(half, dtype=F32) * 2.0 / DK_A)
    ang = pos.astype(F32)[:, None] * inv[None, :]
    cos, sin = jnp.cos(ang), jnp.sin(ang)
    cos = jnp.concatenate([cos, cos, cos, cos], axis=-1)
    sin = jnp.concatenate([-sin, sin, -sin, sin], axis=-1)
    return cos, sin


def _layer_weights(l, w_in, w_uq, w_uk, w_uv, w_oa, w_ob, w_out, norm_in, norm_qa, norm_kva):
    d = w_in.shape[1]
    wi = w_in[l]
    wi = jnp.concatenate([wi[:, :C_KPE + D_ROPE], jnp.zeros((d, LANES - D_ROPE), wi.dtype), wi[:, C_KPE + D_ROPE:]],
                         axis=1)
    uq = w_uq[l]
    uqr = jnp.pad(uq[:, :, D_NOPE:], ((0, 0), (0, 0), (0, LANES - D_ROPE)))
    return {
        "w_in": wi.astype(BF16),
        "w_uqn": uq[:, :, :D_NOPE].reshape(Q_LORA, H_B * D_NOPE).astype(BF16),
        "w_uqr": uqr.reshape(Q_LORA, H_B * LANES).astype(BF16),
        "w_ukt": jnp.transpose(w_uk[l], (1, 2, 0)).astype(BF16),
        "w_uvt": jnp.transpose(w_uv[l], (1, 0, 2)).astype(BF16),
        "w_oa": w_oa[l].astype(BF16),
        "w_ob": w_ob[l].astype(BF16),
        "w_out": w_out[l].astype(BF16),
        "g_in": norm_in[l][None, :],
        "g_qa": norm_qa[l][None, :],
        "g_kva": norm_kva[l][None, :],
    }


def _row_tile(n, cap):
    tm = min(n, cap)
    assert n % tm == 0
    return tm


def kernel(x_prompt, x_sample, cache_diff_k, cache_diff_v, cache_mla_ckv, cache_mla_kpe, w_in, w_uq, w_uk, w_uv,
           w_oa, w_ob, w_out, lambda_q1, lambda_k1, lambda_q2, lambda_k2, norm_in, norm_qa, norm_kva, norm_subln,
           norm_final):
    b, t, d = x_prompt.shape
    bs, ts, _ = x_sample.shape
    depth = w_in.shape[0]
    past_len = cache_diff_k.shape[2]
    tq = _row_tile(t, 512)
    tm_p = _row_tile(t, 512)
    tm_s = _row_tile(bs * ts, 512)
    assert tm_s % ts == 0 and tq % CHUNK == 0 and tq == tm_p

    cos_p, sin_p = _rope_tables(jnp.arange(t, dtype=jnp.int32))
    cos_s, sin_s = _rope_tables(past_len + jnp.arange(ts, dtype=jnp.int32))
    cos_s = jnp.tile(cos_s, (tm_s // ts, 1))
    sin_s = jnp.tile(sin_s, (tm_s // ts, 1))
    g_final = norm_final[None, :]

    hp = x_prompt.reshape(b * t, d)
    hs = x_sample.reshape(bs * ts, d)
    new_p = ([], [], [], [])
    new_s = ([], [], [], [])
    for l in range(depth):
        lam_init = 0.8 - 0.6 * math.exp(-0.3 * l)
        last = l == depth - 1
        wts = _layer_weights(l, w_in, w_uq, w_uk, w_uv, w_oa, w_ob, w_out, norm_in, norm_qa, norm_kva)
        lam_p = jnp.stack([lambda_q1[l], lambda_k1[l], lambda_q2[l], lambda_k2[l]])
        g_sub = norm_subln[l][None, :]

        (q, qt, k, v, kb, vb, vt, za, zb, ma, mb, ckv, kpe, kcat, qcat) = _in_proj(
            hp, cos_p, sin_p, t // tm_p, tm_p, wts)
        oa = _diff_attn(qt.reshape(b, t // tq, W_A, tq), kb.reshape(b, t, W_A), vt.reshape(b, t // tq, W_A, tq),
                        lam_p, norm_subln[l][:, None], lam_init, tq)
        ob = _mla_attn(qcat.reshape(H_B, b, t, KCAT), kcat.reshape(b, t, KCAT), wts["w_uvt"], tq)
        hp = _out_proj(hp, oa.reshape(b * t, W_A), ob.reshape(b * t, W_B), za, zb, ma, mb, wts, g_final, last, tm_p)
        for lst, a in zip(new_p, (k.reshape(b, t, H_A, 2 * DK_A), v.reshape(b, t, H_A, DV_A),
                                  ckv.reshape(b, t, KV_LORA), kpe.reshape(b, t, D_ROPE))):
            lst.append(a)

        (q, _, k, v, kb, vb, _, za, zb, ma, mb, ckv, kpe, kcat, qcat) = _in_proj(hs, cos_s, sin_s, 1, tm_s, wts)
        oa, ob = _sample_attn(
            q.reshape(bs, ts, W_A), cache_diff_k[l].reshape(bs, past_len * H_A, 2 * DK_A),
            cache_diff_v[l].reshape(bs, past_len * H_A, DV_A), kb.reshape(bs, ts, W_A), vb.reshape(bs, ts, W_A),
            qcat.reshape(H_B, bs, ts, KCAT), cache_mla_ckv[l], jnp.swapaxes(cache_mla_kpe[l], 1, 2),
            kcat.reshape(bs, ts, KCAT), lam_p, g_sub, wts["w_uvt"], lam_init)
        hs = _out_proj(hs, oa.reshape(bs * ts, W_A), ob.reshape(bs * ts, W_B), za, zb, ma, mb, wts, g_final, last, tm_s)
        for lst, a in zip(new_s, (k.reshape(bs, ts, H_A, 2 * DK_A), v.reshape(bs, ts, H_A, DV_A),
                                  ckv.reshape(bs, ts, KV_LORA), kpe.reshape(bs, ts, D_ROPE))):
            lst.append(a)

    return (hp.reshape(b, t, d), hs.reshape(bs, ts, d),
            jnp.stack(new_p[0]), jnp.stack(new_p[1]), jnp.stack(new_p[2]), jnp.stack(new_p[3]),
            jnp.stack(new_s[0]), jnp.stack(new_s[1]), jnp.stack(new_s[2]), jnp.stack(new_s[3]))
```

```python
import functools
import math

import jax
import jax.numpy as jnp
from jax import lax
from jax.experimental import pallas as pl
from jax.experimental.pallas import tpu as pltpu

F32 = jnp.float32
BF16 = jnp.bfloat16

CHUNK = 64
ROPE_THETA = 10000.0
EPS = 1e-6
NEG_INF = -1e30
LOG2E = math.log2(math.e)

H_A = 4
DK_A = 64
DV_A = 2 * DK_A
W_A = H_A * DV_A
H_B = 4
D_NOPE = 128
D_ROPE = 64
DV_B = 128
Q_LORA = 512
KV_LORA = 256
W_B = H_B * DV_B
DIFF_SCALE = DK_A ** -0.5 * LOG2E
MLA_SCALE = (D_NOPE + D_ROPE) ** -0.5 * LOG2E

LANES = 128
KCAT = KV_LORA + LANES
VMEM_LIMIT = 56 * 1024 * 1024

C_QA = 0
C_KA = C_QA + 2 * H_A * DK_A
C_VA = C_KA + 2 * H_A * DK_A
C_ZA = C_VA + W_A
C_QD = C_ZA + W_A
C_CKV = C_QD + Q_LORA
C_KPE = C_CKV + KV_LORA
C_ZB = C_KPE + LANES
C_MA = C_ZB + W_B


def _rms(x, g):
    return x * lax.rsqrt(jnp.mean(x * x, axis=-1, keepdims=True) + EPS) * g


def _rope(x, cos, sin_signed):
    w = x.shape[-1]
    lane = lax.broadcasted_iota(jnp.int32, x.shape, 1)
    rot = jnp.where((lane % DK_A) < DK_A // 2, pltpu.roll(x, w - DK_A // 2, 1), pltpu.roll(x, DK_A // 2, 1))
    return x * cos + rot * sin_signed


def _dot(a, b):
    return jnp.dot(a, b, preferred_element_type=F32)


def _dot_nt(a, b):
    return lax.dot_general(a, b, (((1,), (1,)), ((), ())), preferred_element_type=F32)


def _in_proj_kernel(d_model, prompt, x_ref, cos_ref, sin_ref, w_in_ref, g_in_ref, g_qa_ref, g_kva_ref,
                    w_uqn_ref, w_uqr_ref, *refs):
    if prompt:
        (w_ukf_ref, w_uvf_ref, qt_ref, k_ref, v_ref, kb_ref, vt_ref, za_ref, zb_ref, ma_ref, mb_ref, ckv_ref,
         kpe_ref, kh_ref, qht_ref, vht_ref) = refs
    else:
        (w_ukt_ref, q_ref, k_ref, v_ref, kb_ref, vb_ref, za_ref, zb_ref, ma_ref, mb_ref, ckv_ref, kpe_ref,
         kcat_ref, qcat_ref) = refs
    tm = x_ref.shape[0]
    h = _rms(x_ref[...], g_in_ref[...]).astype(BF16)
    cos = cos_ref[...]
    sin = sin_ref[...]

    def proj(c0, n):
        return _dot(h, w_in_ref[:, c0:c0 + n])

    p = proj(C_QA, 2 * H_A * DK_A)
    for j in range(H_A):
        sl = slice(j * LANES, (j + 1) * LANES)
        qr = _rope(p[:, sl], cos, sin) * DIFF_SCALE
        if prompt:
            qt_ref[0, sl, :] = qr.T.astype(BF16)
        else:
            q_ref[:, sl] = qr.astype(BF16)
    p = proj(C_KA, 2 * H_A * DK_A)
    for j in range(H_A):
        sl = slice(j * LANES, (j + 1) * LANES)
        kr = _rope(p[:, sl], cos, sin)
        k_ref[pl.ds(j, tm, stride=H_A), :] = kr
        kb_ref[:, sl] = kr.astype(BF16)
    p = proj(C_VA, W_A)
    for j in range(H_A):
        sl = slice(j * LANES, (j + 1) * LANES)
        v_ref[pl.ds(j, tm, stride=H_A), :] = p[:, sl]
        if prompt:
            vt_ref[0, sl, :] = p[:, sl].T.astype(BF16)
    if not prompt:
        vb_ref[...] = p.astype(BF16)
    za_ref[...] = proj(C_ZA, W_A).astype(BF16)

    qd = _rms(proj(C_QD, Q_LORA), g_qa_ref[...]).astype(BF16)
    qn = _dot(qd, w_uqn_ref[...])
    qr = _dot(qd, w_uqr_ref[...])
    ckv = _rms(proj(C_CKV, KV_LORA), g_kva_ref[...])
    ckv_ref[...] = ckv
    ckv_b = ckv.astype(BF16)
    kp = _rope(proj(C_KPE, LANES), cos, sin)
    kpe_ref[...] = kp[:, 0:D_ROPE]
    if prompt:
        k_nope = _dot(ckv_b, w_ukf_ref[...])
        v_up = _dot(ckv_b, w_uvf_ref[...])
        for j in range(H_B):
            sl = slice(j * LANES, (j + 1) * LANES)
            kh_ref[j, :, 0:D_NOPE] = k_nope[:, sl].astype(BF16)
            kh_ref[j, :, D_NOPE:2 * LANES] = kp.astype(BF16)
            qht_ref[0, j, 0:D_NOPE, :] = (qn[:, sl] * MLA_SCALE).T.astype(BF16)
            qht_ref[0, j, D_NOPE:2 * LANES, :] = (_rope(qr[:, sl], cos, sin) * MLA_SCALE).T.astype(BF16)
            vht_ref[0, sl, :] = v_up[:, sl].T.astype(BF16)
    else:
        qn = qn.astype(BF16)
        for j in range(H_B):
            sl = slice(j * LANES, (j + 1) * LANES)
            q_lat = _dot(qn[:, sl], w_ukt_ref[j])
            qcat_ref[j, :, 0:KV_LORA] = (q_lat * MLA_SCALE).astype(BF16)
            qcat_ref[j, :, KV_LORA:KCAT] = (_rope(qr[:, sl], cos, sin) * MLA_SCALE).astype(BF16)
        kcat_ref[:, 0:KV_LORA] = ckv_b
        kcat_ref[:, KV_LORA:KCAT] = kp.astype(BF16)

    zb_ref[...] = proj(C_ZB, W_B).astype(BF16)
    ma_ref[...] = proj(C_MA, d_model).astype(BF16)
    mb_ref[...] = proj(C_MA + d_model, d_model).astype(BF16)


def _in_proj(x, cos, sin, n_pos_blocks, tm, wts, prompt):
    n, d = x.shape
    d_in = wts["w_in"].shape[1]
    const = lambda shape: pl.BlockSpec(shape, lambda i: (0,) * len(shape))
    row = lambda w: (jax.ShapeDtypeStruct((n, w), BF16), pl.BlockSpec((tm, w), lambda i: (i, 0)))
    row_f32 = lambda w: (jax.ShapeDtypeStruct((n, w), F32), pl.BlockSpec((tm, w), lambda i: (i, 0)))
    heads = (jax.ShapeDtypeStruct((n * H_A, LANES), F32), pl.BlockSpec((tm * H_A, LANES), lambda i: (i, 0)))
    tile_t = (jax.ShapeDtypeStruct((n // tm, W_A, tm), BF16), pl.BlockSpec((1, W_A, tm), lambda i: (i, 0, 0)))
    per_head = lambda w: (jax.ShapeDtypeStruct((H_B, n, w), BF16), pl.BlockSpec((H_B, tm, w), lambda i: (0, i, 0)))
    per_head_t = (jax.ShapeDtypeStruct((n // tm, H_B, 2 * LANES, tm), BF16),
                  pl.BlockSpec((1, H_B, 2 * LANES, tm), lambda i: (i, 0, 0, 0)))
    common = [row(W_A), row(W_B), row(d), row(d), row_f32(KV_LORA), row_f32(D_ROPE)]
    if prompt:
        weights = [wts["w_ukf"], wts["w_uvf"]]
        w_specs = [const((KV_LORA, H_B * D_NOPE)), const((KV_LORA, H_B * DV_B))]
        outs = [tile_t, heads, heads, row(W_A), tile_t] + common + [per_head(2 * LANES), per_head_t, tile_t]
    else:
        weights = [wts["w_ukt"]]
        w_specs = [const((H_B, D_NOPE, KV_LORA))]
        outs = [row(W_A), heads, heads, row(W_A), row(W_A)] + common + [row(KCAT), per_head(KCAT)]
    pos = pl.BlockSpec((tm, LANES), lambda i: (i % n_pos_blocks, 0))
    x_spec = pl.BlockSpec((tm, d), lambda i: (i, 0))
    return pl.pallas_call(
        functools.partial(_in_proj_kernel, d, prompt),
        grid=(n // tm,),
        in_specs=[x_spec, pos, pos, const((d, d_in)), const((1, d)), const((1, Q_LORA)), const((1, KV_LORA)),
                  const((Q_LORA, H_B * D_NOPE)), const((Q_LORA, H_B * LANES))] + w_specs,
        out_specs=tuple(o[1] for o in outs),
        out_shape=tuple(o[0] for o in outs),
        compiler_params=pltpu.CompilerParams(dimension_semantics=("arbitrary",), vmem_limit_bytes=VMEM_LIMIT),
        name="in_proj",
    )(x, cos, sin, wts["w_in"], wts["g_in"], wts["g_qa"], wts["g_kva"], wts["w_uqn"], wts["w_uqr"], *weights)


def _lambda(lam_ref, lam_init):
    lp = lam_ref[...]
    a1 = jnp.sum(lp[0:1] * lp[1:2], axis=1, keepdims=True)
    a2 = jnp.sum(lp[2:3] * lp[3:4], axis=1, keepdims=True)
    return jnp.exp(a1) - jnp.exp(a2) + lam_init


def _stack_diff_queries(q):
    lane = lax.broadcasted_iota(jnp.int32, q.shape, 1)
    zero = jnp.zeros_like(q)
    return jnp.concatenate([jnp.where(lane < DK_A, q, zero), jnp.where(lane >= DK_A, q, zero)], axis=0)


def _diag_mask_t(tq, cols):
    r = lax.broadcasted_iota(jnp.int32, (tq, cols), 0)
    c = lax.broadcasted_iota(jnp.int32, (tq, cols), 1) % tq
    return (r // CHUNK) <= (c // CHUNK)


def _diff_attn_kernel(lam_init, tq, nq, lam_ref, gsub_ref, qt_ref, k_ref, vt_ref, o_ref):
    qi = pl.program_id(1)
    lam = _lambda(lam_ref, lam_init)
    mask = _diag_mask_t(tq, 2 * tq)

    def branch(nb):
        def score_piece(h, j):
            sl = slice(h * LANES, (h + 1) * LANES)
            qt = qt_ref[0, 0, sl, :]
            row = lax.broadcasted_iota(jnp.int32, qt.shape, 0)
            zero = jnp.zeros_like(qt)
            qst = jnp.concatenate([jnp.where(row < DK_A, qt, zero), jnp.where(row >= DK_A, qt, zero)], axis=1)
            sj = _dot(k_ref[0, j * tq:(j + 1) * tq, sl], qst)
            return jnp.where(mask, sj, NEG_INF) if j == nb - 1 else sj

        s_next = [score_piece(0, j) for j in range(nb)]
        for h in range(H_A):
            sl = slice(h * LANES, (h + 1) * LANES)
            s_cur, s_next = s_next, []
            pending = [(h + 1, j) for j in range(nb)] if h + 1 < H_A else []

            def pump(k):
                while pending and len(s_next) * 3 < k:
                    s_next.append(score_piece(*pending.pop(0)))

            m = None
            for j in range(nb):
                mj = jnp.max(s_cur[j], axis=0, keepdims=True)
                m = mj if m is None else jnp.maximum(m, mj)
                pump(j + 1)
            es, l = [], None
            for j in range(nb):
                e = jnp.exp2(s_cur[j] - m)
                es.append(e)
                lj = jnp.sum(e, axis=0, keepdims=True)
                l = lj if l is None else l + lj
                pump(nb + j + 1)
            l1 = l[:, :tq]
            r = lam * l1 / l[:, tq:]
            o = None
            for j in range(nb):
                pj = (es[j][:, :tq] - r * es[j][:, tq:]).astype(BF16)
                oj = _dot(vt_ref[0, j, sl, :], pj)
                o = oj if o is None else o + oj
                pump(2 * nb + j + 1)
            pump(4 * nb)
            od = o / l1
            y = od * lax.rsqrt(jnp.mean(od * od, axis=0, keepdims=True) + EPS) * gsub_ref[...] * (1.0 - lam_init)
            o_ref[0, :, sl] = y.T.astype(BF16)

    for blk in range(nq):
        pl.when(qi == blk)(functools.partial(branch, blk + 1))


def _diff_attn(qt, k, vt, lam_p, g_sub_col, lam_init, tq):
    b, t, _ = k.shape
    nq = t // tq
    return pl.pallas_call(
        functools.partial(_diff_attn_kernel, lam_init, tq, nq),
        grid=(b, nq),
        in_specs=[pl.BlockSpec((4, DK_A), lambda bi, qi: (0, 0)), pl.BlockSpec((DV_A, 1), lambda bi, qi: (0, 0)),
                  pl.BlockSpec((1, 1, W_A, tq), lambda bi, qi: (bi, qi, 0, 0)),
                  pl.BlockSpec((1, t, W_A), lambda bi, qi: (bi, 0, 0)),
                  pl.BlockSpec((1, nq, W_A, tq), lambda bi, qi: (bi, 0, 0, 0))],
        out_specs=pl.BlockSpec((1, tq, W_A), lambda bi, qi: (bi, qi, 0)),
        out_shape=jax.ShapeDtypeStruct((b, t, W_A), BF16),
        compiler_params=pltpu.CompilerParams(dimension_semantics=("arbitrary", "arbitrary"),
                                             vmem_limit_bytes=VMEM_LIMIT),
        name="diff_attn",
    )(lam_p, g_sub_col, qt, k, vt)


def _mla_attn_kernel(tq, nq, qt_ref, kh_ref, vt_ref, o_ref):
    qi = pl.program_id(1)
    mask = _diag_mask_t(tq, tq)

    def branch(nb):
        def score_piece(h, j):
            sj = _dot(kh_ref[h, 0, j * tq:(j + 1) * tq, :], qt_ref[0, 0, h])
            return jnp.where(mask, sj, NEG_INF) if j == nb - 1 else sj

        s_next = [score_piece(0, j) for j in range(nb)]
        for h in range(H_B):
            sl = slice(h * DV_B, (h + 1) * DV_B)
            s_cur, s_next = s_next, []
            pending = [(h + 1, j) for j in range(nb)] if h + 1 < H_B else []

            def pump(k):
                while pending and len(s_next) * 2 < k:
                    s_next.append(score_piece(*pending.pop(0)))

            m = None
            for j in range(nb):
                mj = jnp.max(s_cur[j], axis=0, keepdims=True)
                m = mj if m is None else jnp.maximum(m, mj)
                pump(j + 1)
            o, l = None, None
            for j in range(nb):
                e = jnp.exp2(s_cur[j] - m)
                lj = jnp.sum(e, axis=0, keepdims=True)
                l = lj if l is None else l + lj
                oj = _dot(vt_ref[0, j, sl, :], e.astype(BF16))
                o = oj if o is None else o + oj
                pump(nb + j + 1)
            pump(4 * nb)
            o_ref[0, :, sl] = (o / l).T.astype(BF16)

    for blk in range(nq):
        pl.when(qi == blk)(functools.partial(branch, blk + 1))


def _mla_attn(qht, kh, vht, tq):
    _, b, t, kw = kh.shape
    nq = t // tq
    return pl.pallas_call(
        functools.partial(_mla_attn_kernel, tq, nq),
        grid=(b, nq),
        in_specs=[pl.BlockSpec((1, 1, H_B, kw, tq), lambda bi, qi: (bi, qi, 0, 0, 0)),
                  pl.BlockSpec((H_B, 1, t, kw), lambda bi, qi: (0, bi, 0, 0)),
                  pl.BlockSpec((1, nq, W_B, tq), lambda bi, qi: (bi, 0, 0, 0))],
        out_specs=pl.BlockSpec((1, tq, W_B), lambda bi, qi: (bi, qi, 0)),
        out_shape=jax.ShapeDtypeStruct((b, t, W_B), BF16),
        compiler_params=pltpu.CompilerParams(dimension_semantics=("arbitrary", "arbitrary"),
                                             vmem_limit_bytes=VMEM_LIMIT),
        name="mla_attn",
    )(qht, kh, vht)


def _two_part_softmax(s_past, s_new, v_past, v_new):
    m = jnp.maximum(jnp.max(s_past, axis=1, keepdims=True), jnp.max(s_new, axis=1, keepdims=True))
    p_past = jnp.exp2(s_past - m)
    p_new = jnp.exp2(s_new - m)
    l = jnp.sum(p_past, axis=1, keepdims=True) + jnp.sum(p_new, axis=1, keepdims=True)
    return (_dot(p_past.astype(BF16), v_past) + _dot(p_new.astype(BF16), v_new)) / l


def _sample_attn_kernel(lam_init, past_len, ts, lam_ref, gsub_ref, w_uvt_ref, q_ref, ck_ref, cv_ref, kn_ref, vn_ref,
                        qcat_ref, cckv_ref, ckpe_t_ref, kcn_ref, oa_ref, ob_ref):
    def new_mask(rows):
        r = past_len + lax.broadcasted_iota(jnp.int32, (rows, ts), 0) % ts
        c = past_len + lax.broadcasted_iota(jnp.int32, (rows, ts), 1)
        return (c // CHUNK) <= (r // CHUNK)

    lam = _lambda(lam_ref, lam_init)
    mask2 = new_mask(2 * ts)
    for h in range(H_A):
        sl = slice(h * LANES, (h + 1) * LANES)
        qs = _stack_diff_queries(q_ref[0, :, sl])
        head_rows = pl.ds(h, past_len, stride=H_A)
        s_past = _dot_nt(qs, ck_ref[0, head_rows, :].astype(BF16))
        s_new = jnp.where(mask2, _dot_nt(qs, kn_ref[0, :, sl]), NEG_INF)
        o = _two_part_softmax(s_past, s_new, cv_ref[0, head_rows, :].astype(BF16), vn_ref[0, :, sl])
        od = o[:ts] - lam * o[ts:]
        oa_ref[0, :, sl] = (_rms(od, gsub_ref[...]) * (1.0 - lam_init)).astype(BF16)

    qs = qcat_ref[:, 0].reshape(H_B * ts, KCAT)
    ckv = cckv_ref[0].astype(BF16)
    kc_new = kcn_ref[0]
    s_past = (_dot_nt(qs[:, 0:KV_LORA], ckv)
              + _dot(qs[:, KV_LORA:KV_LORA + D_ROPE], ckpe_t_ref[0].astype(BF16)))
    s_new = jnp.where(new_mask(H_B * ts), _dot_nt(qs, kc_new), NEG_INF)
    o_lat = _two_part_softmax(s_past, s_new, ckv, kc_new[:, 0:KV_LORA]).astype(BF16)
    for h in range(H_B):
        ob_ref[0, :, h * DV_B:(h + 1) * DV_B] = _dot(o_lat[h * ts:(h + 1) * ts], w_uvt_ref[h]).astype(BF16)


def _sample_attn(q, ck, cv, kn, vn, qcat, cckv, ckpe_t, kcn, lam_p, g_sub, w_uvt, lam_init):
    b, ts, _ = q.shape
    past_len = cckv.shape[1]
    per_b = lambda rows, w: pl.BlockSpec((1, rows, w), lambda bi: (bi, 0, 0))
    const = lambda shape: pl.BlockSpec(shape, lambda bi: (0,) * len(shape))
    return pl.pallas_call(
        functools.partial(_sample_attn_kernel, lam_init, past_len, ts),
        grid=(b,),
        in_specs=[const((4, DK_A)), const((1, DV_A)), const((H_B, KV_LORA, DV_B)),
                  per_b(ts, W_A), per_b(past_len * H_A, LANES), per_b(past_len * H_A, LANES),
                  per_b(ts, W_A), per_b(ts, W_A),
                  pl.BlockSpec((H_B, 1, ts, KCAT), lambda bi: (0, bi, 0, 0)),
                  per_b(past_len, KV_LORA), per_b(D_ROPE, past_len), per_b(ts, KCAT)],
        out_specs=(per_b(ts, W_A), per_b(ts, W_B)),
        out_shape=(jax.ShapeDtypeStruct((b, ts, W_A), BF16), jax.ShapeDtypeStruct((b, ts, W_B), BF16)),
        compiler_params=pltpu.CompilerParams(dimension_semantics=("arbitrary",), vmem_limit_bytes=VMEM_LIMIT),
        name="sample_attn",
    )(lam_p, g_sub, w_uvt, q, ck, cv, kn, vn, qcat, cckv, ckpe_t, kcn)


def _out_proj_kernel(final_norm, x_ref, oa_ref, ob_ref, za_ref, zb_ref, ma_ref, mb_ref,
                     w_oa_ref, w_ob_ref, w_out_ref, g_fin_ref, y_ref):
    a = (oa_ref[...].astype(F32) * jax.nn.silu(za_ref[...].astype(F32))).astype(BF16)
    y_a = _dot(a, w_oa_ref[...])
    bb = (ob_ref[...].astype(F32) * jax.nn.silu(zb_ref[...].astype(F32))).astype(BF16)
    y_b = _dot(bb, w_ob_ref[...])
    merged = jax.nn.sigmoid(ma_ref[...].astype(F32)) * y_a + jax.nn.sigmoid(mb_ref[...].astype(F32)) * y_b
    out = x_ref[...] + _dot(merged.astype(BF16), w_out_ref[...])
    y_ref[...] = _rms(out, g_fin_ref[...]) if final_norm else out


def _out_proj(x, oa, ob, za, zb, ma, mb, wts, g_final, final_norm, tm):
    n, d = x.shape
    const = lambda shape: pl.BlockSpec(shape, lambda i: (0,) * len(shape))
    row = lambda w: pl.BlockSpec((tm, w), lambda i: (i, 0))
    return pl.pallas_call(
        functools.partial(_out_proj_kernel, final_norm),
        grid=(n // tm,),
        in_specs=[row(d), row(W_A), row(W_B), row(W_A), row(W_B), row(d), row(d),
                  const((W_A, d)), const((W_B, d)), const((d, d)), const((1, d))],
        out_specs=row(d),
        out_shape=jax.ShapeDtypeStruct((n, d), F32),
        compiler_params=pltpu.CompilerParams(dimension_semantics=("arbitrary",), vmem_limit_bytes=VMEM_LIMIT),
        name="out_proj",
    )(x, oa, ob, za, zb, ma, mb, wts["w_oa"], wts["w_ob"], wts["w_out"], g_final)


def _rope_tables(pos):
    half = DK_A // 2
    inv = ROPE_THETA ** (-jnp.arange(half, dtype=F32) * 2.0 / DK_A)
    ang = pos.astype(F32)[:, None] * inv[None, :]
    cos, sin = jnp.cos(ang), jnp.sin(ang)
    cos = jnp.concatenate([cos, cos, cos, cos], axis=-1)
    sin = jnp.concatenate([-sin, sin, -sin, sin], axis=-1)
    return cos, sin


def _layer_weights(l, w_in, w_uq, w_uk, w_uv, w_oa, w_ob, w_out, norm_in, norm_qa, norm_kva):
    d = w_in.shape[1]
    wi = w_in[l]
    wi = jnp.concatenate([wi[:, :C_KPE + D_ROPE], jnp.zeros((d, LANES - D_ROPE), wi.dtype), wi[:, C_KPE + D_ROPE:]],
                         axis=1)
    uq = w_uq[l]
    uqr = jnp.pad(uq[:, :, D_NOPE:], ((0, 0), (0, 0), (0, LANES - D_ROPE)))
    return {
        "w_in": wi.astype(BF16),
        "w_uqn": uq[:, :, :D_NOPE].reshape(Q_LORA, H_B * D_NOPE).astype(BF16),
        "w_uqr": uqr.reshape(Q_LORA, H_B * LANES).astype(BF16),
        "w_ukt": jnp.transpose(w_uk[l], (1, 2, 0)).astype(BF16),
        "w_ukf": w_uk[l].reshape(KV_LORA, H_B * D_NOPE).astype(BF16),
        "w_uvf": w_uv[l].reshape(KV_LORA, H_B * DV_B).astype(BF16),
        "w_uvt": jnp.transpose(w_uv[l], (1, 0, 2)).astype(BF16),
        "w_oa": w_oa[l].astype(BF16),
        "w_ob": w_ob[l].astype(BF16),
        "w_out": w_out[l].astype(BF16),
        "g_in": norm_in[l][None, :],
        "g_qa": norm_qa[l][None, :],
        "g_kva": norm_kva[l][None, :],
    }


def _row_tile(n, cap):
    tm = min(n, cap)
    assert n % tm == 0
    return tm


def kernel(x_prompt, x_sample, cache_diff_k, cache_diff_v, cache_mla_ckv, cache_mla_kpe, w_in, w_uq, w_uk, w_uv,
           w_oa, w_ob, w_out, lambda_q1, lambda_k1, lambda_q2, lambda_k2, norm_in, norm_qa, norm_kva, norm_subln,
           norm_final):
    b, t, d = x_prompt.shape
    bs, ts, _ = x_sample.shape
    depth = w_in.shape[0]
    past_len = cache_diff_k.shape[2]
    tq = _row_tile(t, 512)
    tm_p = _row_tile(t, 512)
    tm_s = _row_tile(bs * ts, 512)
    assert tm_s % ts == 0 and tq % CHUNK == 0 and tq == tm_p

    cos_p, sin_p = _rope_tables(jnp.arange(t, dtype=jnp.int32))
    cos_s, sin_s = _rope_tables(past_len + jnp.arange(ts, dtype=jnp.int32))
    cos_s = jnp.tile(cos_s, (tm_s // ts, 1))
    sin_s = jnp.tile(sin_s, (tm_s // ts, 1))
    g_final = norm_final[None, :]

    hp = x_prompt.reshape(b * t, d)
    hs = x_sample.reshape(bs * ts, d)
    new_p = ([], [], [], [])
    new_s = ([], [], [], [])
    for l in range(depth):
        lam_init = 0.8 - 0.6 * math.exp(-0.3 * l)
        last = l == depth - 1
        wts = _layer_weights(l, w_in, w_uq, w_uk, w_uv, w_oa, w_ob, w_out, norm_in, norm_qa, norm_kva)
        lam_p = jnp.stack([lambda_q1[l], lambda_k1[l], lambda_q2[l], lambda_k2[l]])
        g_sub = norm_subln[l][None, :]

        (qt, k, v, kb, vt, za, zb, ma, mb, ckv, kpe, kh, qht, vht) = _in_proj(
            hp, cos_p, sin_p, t // tm_p, tm_p, wts, True)
        oa = _diff_attn(qt.reshape(b, t // tq, W_A, tq), kb.reshape(b, t, W_A), vt.reshape(b, t // tq, W_A, tq),
                        lam_p, norm_subln[l][:, None], lam_init, tq)
        ob = _mla_attn(qht.reshape(b, t // tq, H_B, 2 * LANES, tq), kh.reshape(H_B, b, t, 2 * LANES),
                       vht.reshape(b, t // tq, W_B, tq), tq)
        hp = _out_proj(hp, oa.reshape(b * t, W_A), ob.reshape(b * t, W_B), za, zb, ma, mb, wts, g_final, last, tm_p)
        for lst, a in zip(new_p, (k.reshape(b, t, H_A, 2 * DK_A), v.reshape(b, t, H_A, DV_A),
                                  ckv.reshape(b, t, KV_LORA), kpe.reshape(b, t, D_ROPE))):
            lst.append(a)

        (q, k, v, kb, vb, za, zb, ma, mb, ckv, kpe, kcat, qcat) = _in_proj(hs, cos_s, sin_s, 1, tm_s, wts, False)
        oa, ob = _sample_attn(
            q.reshape(bs, ts, W_A), cache_diff_k[l].reshape(bs, past_len * H_A, 2 * DK_A),
            cache_diff_v[l].reshape(bs, past_len * H_A, DV_A), kb.reshape(bs, ts, W_A), vb.reshape(bs, ts, W_A),
            qcat.reshape(H_B, bs, ts, KCAT), cache_mla_ckv[l], jnp.swapaxes(cache_mla_kpe[l], 1, 2),
            kcat.reshape(bs, ts, KCAT), lam_p, g_sub, wts["w_uvt"], lam_init)
        hs = _out_proj(hs, oa.reshape(bs * ts, W_A), ob.reshape(bs * ts, W_B), za, zb, ma, mb, wts, g_final, last, tm_s)
        for lst, a in zip(new_s, (k.reshape(bs, ts, H_A, 2 * DK_A), v.reshape(bs, ts, H_A, DV_A),
                                  ckv.reshape(bs, ts, KV_LORA), kpe.reshape(bs, ts, D_ROPE))):
            lst.append(a)

    return (hp.reshape(b, t, d), hs.reshape(bs, ts, d),
            jnp.stack(new_p[0]), jnp.stack(new_p[1]), jnp.stack(new_p[2]), jnp.stack(new_p[3]),
            jnp.stack(new_s[0]), jnp.stack(new_s[1]), jnp.stack(new_s[2]), jnp.stack(new_s[3]))
```

```python
import functools
import math

import jax
import jax.numpy as jnp
from jax import lax
from jax.experimental import pallas as pl
from jax.experimental.pallas import tpu as pltpu

F32 = jnp.float32
BF16 = jnp.bfloat16

CHUNK = 64
ROPE_THETA = 10000.0
EPS = 1e-6
NEG_INF = -1e30
LOG2E = math.log2(math.e)

H_A = 4
DK_A = 64
DV_A = 2 * DK_A
W_A = H_A * DV_A
H_B = 4
D_NOPE = 128
D_ROPE = 64
DV_B = 128
Q_LORA = 512
KV_LORA = 256
W_B = H_B * DV_B
DIFF_SCALE = DK_A ** -0.5 * LOG2E
MLA_SCALE = (D_NOPE + D_ROPE) ** -0.5 * LOG2E

LANES = 128
KCAT = KV_LORA + LANES
N_SUB = 2
VMEM_LIMIT = 56 * 1024 * 1024

C_QA = 0
C_KA = C_QA + 2 * H_A * DK_A
C_VA = C_KA + 2 * H_A * DK_A
C_ZA = C_VA + W_A
C_QD = C_ZA + W_A
C_CKV = C_QD + Q_LORA
C_KPE = C_CKV + KV_LORA
C_ZB = C_KPE + LANES
C_MA = C_ZB + W_B


def _rms(x, g):
    return x * lax.rsqrt(jnp.mean(x * x, axis=-1, keepdims=True) + EPS) * g


def _rope(x, cos, sin_signed):
    w = x.shape[-1]
    lane = lax.broadcasted_iota(jnp.int32, x.shape, 1)
    rot = jnp.where((lane % DK_A) < DK_A // 2, pltpu.roll(x, w - DK_A // 2, 1), pltpu.roll(x, DK_A // 2, 1))
    return x * cos + rot * sin_signed


def _dot(a, b):
    return jnp.dot(a, b, preferred_element_type=F32)


def _dot_nt(a, b):
    return lax.dot_general(a, b, (((1,), (1,)), ((), ())), preferred_element_type=F32)


def _in_proj_kernel(d_model, prompt, x_ref, cos_ref, sin_ref, w_in_ref, g_in_ref, g_qa_ref, g_kva_ref,
                    w_uqn_ref, w_uqr_ref, *refs):
    if prompt:
        (w_ukf_ref, w_uvf_ref, qt_ref, k_ref, v_ref, kb_ref, vt_ref, za_ref, zb_ref, ma_ref, mb_ref, ckv_ref,
         kpe_ref, kh_ref, qht_ref, vht_ref) = refs
    else:
        (w_ukt_ref, q_ref, k_ref, v_ref, kb_ref, vb_ref, za_ref, zb_ref, ma_ref, mb_ref, ckv_ref, kpe_ref,
         kcat_ref, qcat_ref) = refs
    tm = x_ref.shape[0]
    h = _rms(x_ref[...], g_in_ref[...]).astype(BF16)
    cos = cos_ref[...]
    sin = sin_ref[...]

    def proj(c0, n):
        return _dot(h, w_in_ref[:, c0:c0 + n])

    p = proj(C_QA, 2 * H_A * DK_A)
    for j in range(H_A):
        sl = slice(j * LANES, (j + 1) * LANES)
        qr = _rope(p[:, sl], cos, sin) * DIFF_SCALE
        if prompt:
            qt_ref[0, sl, :] = qr.T.astype(BF16)
        else:
            q_ref[:, sl] = qr.astype(BF16)
    p = proj(C_KA, 2 * H_A * DK_A)
    for j in range(H_A):
        sl = slice(j * LANES, (j + 1) * LANES)
        kr = _rope(p[:, sl], cos, sin)
        k_ref[pl.ds(j, tm, stride=H_A), :] = kr
        kb_ref[:, sl] = kr.astype(BF16)
    p = proj(C_VA, W_A)
    for j in range(H_A):
        sl = slice(j * LANES, (j + 1) * LANES)
        v_ref[pl.ds(j, tm, stride=H_A), :] = p[:, sl]
        if prompt:
            vt_ref[0, sl, :] = p[:, sl].T.astype(BF16)
    if not prompt:
        vb_ref[...] = p.astype(BF16)
    za_ref[...] = proj(C_ZA, W_A).astype(BF16)

    qd = _rms(proj(C_QD, Q_LORA), g_qa_ref[...]).astype(BF16)
    qn = _dot(qd, w_uqn_ref[...])
    qr = _dot(qd, w_uqr_ref[...])
    ckv = _rms(proj(C_CKV, KV_LORA), g_kva_ref[...])
    ckv_ref[...] = ckv
    ckv_b = ckv.astype(BF16)
    kp = _rope(proj(C_KPE, LANES), cos, sin)
    kpe_ref[...] = kp[:, 0:D_ROPE]
    if prompt:
        k_nope = _dot(ckv_b, w_ukf_ref[...])
        v_up = _dot(ckv_b, w_uvf_ref[...])
        for j in range(H_B):
            sl = slice(j * LANES, (j + 1) * LANES)
            kh_ref[j, :, 0:D_NOPE] = k_nope[:, sl].astype(BF16)
            kh_ref[j, :, D_NOPE:2 * LANES] = kp.astype(BF16)
            qht_ref[0, j, 0:D_NOPE, :] = (qn[:, sl] * MLA_SCALE).T.astype(BF16)
            qht_ref[0, j, D_NOPE:2 * LANES, :] = (_rope(qr[:, sl], cos, sin) * MLA_SCALE).T.astype(BF16)
            vht_ref[0, sl, :] = v_up[:, sl].T.astype(BF16)
    else:
        qn = qn.astype(BF16)
        for j in range(H_B):
            sl = slice(j * LANES, (j + 1) * LANES)
            q_lat = _dot(qn[:, sl], w_ukt_ref[j])
            qcat_ref[j, :, 0:KV_LORA] = (q_lat * MLA_SCALE).astype(BF16)
            qcat_ref[j, :, KV_LORA:KCAT] = (_rope(qr[:, sl], cos, sin) * MLA_SCALE).astype(BF16)
        kcat_ref[:, 0:KV_LORA] = ckv_b
        kcat_ref[:, KV_LORA:KCAT] = kp.astype(BF16)

    zb_ref[...] = proj(C_ZB, W_B).astype(BF16)
    ma_ref[...] = proj(C_MA, d_model).astype(BF16)
    mb_ref[...] = proj(C_MA + d_model, d_model).astype(BF16)


def _in_proj(x, cos, sin, n_pos_blocks, tm, wts, prompt):
    n, d = x.shape
    d_in = wts["w_in"].shape[1]
    const = lambda shape: pl.BlockSpec(shape, lambda i: (0,) * len(shape))
    row = lambda w: (jax.ShapeDtypeStruct((n, w), BF16), pl.BlockSpec((tm, w), lambda i: (i, 0)))
    row_f32 = lambda w: (jax.ShapeDtypeStruct((n, w), F32), pl.BlockSpec((tm, w), lambda i: (i, 0)))
    heads = (jax.ShapeDtypeStruct((n * H_A, LANES), F32), pl.BlockSpec((tm * H_A, LANES), lambda i: (i, 0)))
    tile_t = (jax.ShapeDtypeStruct((n // tm, W_A, tm), BF16), pl.BlockSpec((1, W_A, tm), lambda i: (i, 0, 0)))
    per_head = lambda w: (jax.ShapeDtypeStruct((H_B, n, w), BF16), pl.BlockSpec((H_B, tm, w), lambda i: (0, i, 0)))
    per_head_t = (jax.ShapeDtypeStruct((n // tm, H_B, 2 * LANES, tm), BF16),
                  pl.BlockSpec((1, H_B, 2 * LANES, tm), lambda i: (i, 0, 0, 0)))
    common = [row(W_A), row(W_B), row(d), row(d), row_f32(KV_LORA), row_f32(D_ROPE)]
    if prompt:
        weights = [wts["w_ukf"], wts["w_uvf"]]
        w_specs = [const((KV_LORA, H_B * D_NOPE)), const((KV_LORA, H_B * DV_B))]
        outs = [tile_t, heads, heads, row(W_A), tile_t] + common + [per_head(2 * LANES), per_head_t, tile_t]
    else:
        weights = [wts["w_ukt"]]
        w_specs = [const((H_B, D_NOPE, KV_LORA))]
        outs = [row(W_A), heads, heads, row(W_A), row(W_A)] + common + [row(KCAT), per_head(KCAT)]
    pos = pl.BlockSpec((tm, LANES), lambda i: (i % n_pos_blocks, 0))
    x_spec = pl.BlockSpec((tm, d), lambda i: (i, 0))
    return pl.pallas_call(
        functools.partial(_in_proj_kernel, d, prompt),
        grid=(n // tm,),
        in_specs=[x_spec, pos, pos, const((d, d_in)), const((1, d)), const((1, Q_LORA)), const((1, KV_LORA)),
                  const((Q_LORA, H_B * D_NOPE)), const((Q_LORA, H_B * LANES))] + w_specs,
        out_specs=tuple(o[1] for o in outs),
        out_shape=tuple(o[0] for o in outs),
        compiler_params=pltpu.CompilerParams(dimension_semantics=("arbitrary",), vmem_limit_bytes=VMEM_LIMIT),
        name="in_proj",
    )(x, cos, sin, wts["w_in"], wts["g_in"], wts["g_qa"], wts["g_kva"], wts["w_uqn"], wts["w_uqr"], *weights)


def _lambda(lam_ref, lam_init):
    lp = lam_ref[...]
    a1 = jnp.sum(lp[0:1] * lp[1:2], axis=1, keepdims=True)
    a2 = jnp.sum(lp[2:3] * lp[3:4], axis=1, keepdims=True)
    return jnp.exp(a1) - jnp.exp(a2) + lam_init


def _stack_diff_queries(q):
    lane = lax.broadcasted_iota(jnp.int32, q.shape, 1)
    zero = jnp.zeros_like(q)
    return jnp.concatenate([jnp.where(lane < DK_A, q, zero), jnp.where(lane >= DK_A, q, zero)], axis=0)


def _key_pieces(blk, sub, tq, ts):
    return [(j * tq, (j + 1) * tq, False) for j in range(blk)] + [(blk * tq, blk * tq + (sub + 1) * ts, True)]


def _own_block_mask(rows, cols, sub, ts):
    r = lax.broadcasted_iota(jnp.int32, (rows, cols), 0)
    c = sub * ts + lax.broadcasted_iota(jnp.int32, (rows, cols), 1) % ts
    return (r // CHUNK) <= (c // CHUNK)


def _attention_units(nq, n_heads, n_sub):
    return [(blk, h, sub) for blk in range(nq) for h in range(n_heads) for sub in range(n_sub)]


def _run_units(units, n_pieces, score_piece, vector_work):
    s_next = [score_piece(units[0], j) for j in range(n_pieces(units[0]))]
    for i, unit in enumerate(units):
        s_cur, s_next = s_next, []
        nxt = units[i + 1] if i + 1 < len(units) else None
        n_next = n_pieces(nxt) if nxt is not None else 0
        pending = list(range(n_next))

        def pump(done, total):
            while pending and len(s_next) * total < done * n_next:
                s_next.append(score_piece(nxt, pending.pop(0)))

        vector_work(unit, s_cur, pump)
        pump(1, 1)


def _diff_attn_kernel(lam_init, tq, nq, lam_ref, gsub_ref, qt_ref, k_ref, vt_ref, o_ref):
    lam = _lambda(lam_ref, lam_init)
    ts = tq // N_SUB
    masks = [_own_block_mask((sub + 1) * ts, 2 * ts, sub, ts) for sub in range(N_SUB)]

    def score_piece(unit, j):
        blk, h, sub = unit
        sl = slice(h * LANES, (h + 1) * LANES)
        lo, hi, masked = _key_pieces(blk, sub, tq, ts)[j]
        qt = qt_ref[0, blk, sl, sub * ts:(sub + 1) * ts]
        row = lax.broadcasted_iota(jnp.int32, qt.shape, 0)
        zero = jnp.zeros_like(qt)
        qst = jnp.concatenate([jnp.where(row < DK_A, qt, zero), jnp.where(row >= DK_A, qt, zero)], axis=1)
        sj = _dot(k_ref[0, lo:hi, sl], qst)
        return jnp.where(masks[sub], sj, NEG_INF) if masked else sj

    def vector_work(unit, s, pump):
        blk, h, sub = unit
        sl = slice(h * LANES, (h + 1) * LANES)
        pieces = _key_pieces(blk, sub, tq, ts)
        n, stages = len(pieces), 3 * len(pieces)
        m = None
        for j in range(n):
            mj = jnp.max(s[j], axis=0, keepdims=True)
            m = mj if m is None else jnp.maximum(m, mj)
            pump(j + 1, stages)
        es, l = [], None
        for j in range(n):
            e = jnp.exp2(s[j] - m)
            es.append(e)
            lj = jnp.sum(e, axis=0, keepdims=True)
            l = lj if l is None else l + lj
            pump(n + j + 1, stages)
        l1 = l[:, :ts]
        r = lam * l1 / l[:, ts:]
        o = None
        for j, (lo, hi, _) in enumerate(pieces):
            pj = (es[j][:, :ts] - r * es[j][:, ts:]).astype(BF16)
            oj = _dot(vt_ref[0, lo // tq, sl, 0:hi - lo], pj)
            o = oj if o is None else o + oj
            pump(2 * n + j + 1, stages)
        od = o / l1
        y = od * lax.rsqrt(jnp.mean(od * od, axis=0, keepdims=True) + EPS) * gsub_ref[...] * (1.0 - lam_init)
        q0 = blk * tq + sub * ts
        o_ref[0, q0:q0 + ts, sl] = y.T.astype(BF16)

    _run_units(_attention_units(nq, H_A, N_SUB), lambda u: u[0] + 1, score_piece, vector_work)


def _diff_attn(qt, k, vt, lam_p, g_sub_col, lam_init, tq):
    b, t, _ = k.shape
    nq = t // tq
    return pl.pallas_call(
        functools.partial(_diff_attn_kernel, lam_init, tq, nq),
        grid=(b,),
        in_specs=[pl.BlockSpec((4, DK_A), lambda bi: (0, 0)), pl.BlockSpec((DV_A, 1), lambda bi: (0, 0)),
                  pl.BlockSpec((1, nq, W_A, tq), lambda bi: (bi, 0, 0, 0)),
                  pl.BlockSpec((1, t, W_A), lambda bi: (bi, 0, 0)),
                  pl.BlockSpec((1, nq, W_A, tq), lambda bi: (bi, 0, 0, 0))],
        out_specs=pl.BlockSpec((1, t, W_A), lambda bi: (bi, 0, 0)),
        out_shape=jax.ShapeDtypeStruct((b, t, W_A), BF16),
        compiler_params=pltpu.CompilerParams(dimension_semantics=("arbitrary",), vmem_limit_bytes=VMEM_LIMIT),
        name="diff_attn",
    )(lam_p, g_sub_col, qt, k, vt)


def _mla_attn_kernel(tq, nq, qt_ref, kh_ref, vt_ref, o_ref):
    ts = tq // N_SUB
    masks = [_own_block_mask((sub + 1) * ts, ts, sub, ts) for sub in range(N_SUB)]

    def score_piece(unit, j):
        blk, h, sub = unit
        lo, hi, masked = _key_pieces(blk, sub, tq, ts)[j]
        sj = _dot(kh_ref[h, 0, lo:hi, :], qt_ref[0, blk, h, :, sub * ts:(sub + 1) * ts])
        return jnp.where(masks[sub], sj, NEG_INF) if masked else sj

    def vector_work(unit, s, pump):
        blk, h, sub = unit
        sl = slice(h * DV_B, (h + 1) * DV_B)
        pieces = _key_pieces(blk, sub, tq, ts)
        n, stages = len(pieces), 2 * len(pieces)
        m = None
        for j in range(n):
            mj = jnp.max(s[j], axis=0, keepdims=True)
            m = mj if m is None else jnp.maximum(m, mj)
            pump(j + 1, stages)
        o, l = None, None
        for j, (lo, hi, _) in enumerate(pieces):
            e = jnp.exp2(s[j] - m)
            lj = jnp.sum(e, axis=0, keepdims=True)
            l = lj if l is None else l + lj
            oj = _dot(vt_ref[0, lo // tq, sl, 0:hi - lo], e.astype(BF16))
            o = oj if o is None else o + oj
            pump(n + j + 1, stages)
        q0 = blk * tq + sub * ts
        o_ref[0, q0:q0 + ts, sl] = (o / l).T.astype(BF16)

    _run_units(_attention_units(nq, H_B, N_SUB), lambda u: u[0] + 1, score_piece, vector_work)


def _mla_attn(qht, kh, vht, tq):
    _, b, t, kw = kh.shape
    nq = t // tq
    return pl.pallas_call(
        functools.partial(_mla_attn_kernel, tq, nq),
        grid=(b,),
        in_specs=[pl.BlockSpec((1, nq, H_B, kw, tq), lambda bi: (bi, 0, 0, 0, 0)),
                  pl.BlockSpec((H_B, 1, t, kw), lambda bi: (0, bi, 0, 0)),
                  pl.BlockSpec((1, nq, W_B, tq), lambda bi: (bi, 0, 0, 0))],
        out_specs=pl.BlockSpec((1, t, W_B), lambda bi: (bi, 0, 0)),
        out_shape=jax.ShapeDtypeStruct((b, t, W_B), BF16),
        compiler_params=pltpu.CompilerParams(dimension_semantics=("arbitrary",), vmem_limit_bytes=VMEM_LIMIT),
        name="mla_attn",
    )(qht, kh, vht)


def _two_part_softmax(s_past, s_new, v_past, v_new):
    m = jnp.maximum(jnp.max(s_past, axis=1, keepdims=True), jnp.max(s_new, axis=1, keepdims=True))
    p_past = jnp.exp2(s_past - m)
    p_new = jnp.exp2(s_new - m)
    l = jnp.sum(p_past, axis=1, keepdims=True) + jnp.sum(p_new, axis=1, keepdims=True)
    return (_dot(p_past.astype(BF16), v_past) + _dot(p_new.astype(BF16), v_new)) / l


def _sample_attn_kernel(lam_init, past_len, ts, lam_ref, gsub_ref, w_uvt_ref, q_ref, ck_ref, cv_ref, kn_ref, vn_ref,
                        qcat_ref, cckv_ref, ckpe_t_ref, kcn_ref, oa_ref, ob_ref):
    def new_mask(rows):
        r = past_len + lax.broadcasted_iota(jnp.int32, (rows, ts), 0) % ts
        c = past_len + lax.broadcasted_iota(jnp.int32, (rows, ts), 1)
        return (c // CHUNK) <= (r // CHUNK)

    lam = _lambda(lam_ref, lam_init)
    mask2 = new_mask(2 * ts)
    for h in range(H_A):
        sl = slice(h * LANES, (h + 1) * LANES)
        qs = _stack_diff_queries(q_ref[0, :, sl])
        head_rows = pl.ds(h, past_len, stride=H_A)
        s_past = _dot_nt(qs, ck_ref[0, head_rows, :].astype(BF16))
        s_new = jnp.where(mask2, _dot_nt(qs, kn_ref[0, :, sl]), NEG_INF)
        o = _two_part_softmax(s_past, s_new, cv_ref[0, head_rows, :].astype(BF16), vn_ref[0, :, sl])
        od = o[:ts] - lam * o[ts:]
        oa_ref[0, :, sl] = (_rms(od, gsub_ref[...]) * (1.0 - lam_init)).astype(BF16)

    qs = qcat_ref[:, 0].reshape(H_B * ts, KCAT)
    ckv = cckv_ref[0].astype(BF16)
    kc_new = kcn_ref[0]
    s_past = (_dot_nt(qs[:, 0:KV_LORA], ckv)
              + _dot(qs[:, KV_LORA:KV_LORA + D_ROPE], ckpe_t_ref[0].astype(BF16)))
    s_new = jnp.where(new_mask(H_B * ts), _dot_nt(qs, kc_new), NEG_INF)
    o_lat = _two_part_softmax(s_past, s_new, ckv, kc_new[:, 0:KV_LORA]).astype(BF16)
    for h in range(H_B):
        ob_ref[0, :, h * DV_B:(h + 1) * DV_B] = _dot(o_lat[h * ts:(h + 1) * ts], w_uvt_ref[h]).astype(BF16)


def _sample_attn(q, ck, cv, kn, vn, qcat, cckv, ckpe_t, kcn, lam_p, g_sub, w_uvt, lam_init):
    b, ts, _ = q.shape
    past_len = cckv.shape[1]
    per_b = lambda rows, w: pl.BlockSpec((1, rows, w), lambda bi: (bi, 0, 0))
    const = lambda shape: pl.BlockSpec(shape, lambda bi: (0,) * len(shape))
    return pl.pallas_call(
        functools.partial(_sample_attn_kernel, lam_init, past_len, ts),
        grid=(b,),
        in_specs=[const((4, DK_A)), const((1, DV_A)), const((H_B, KV_LORA, DV_B)),
                  per_b(ts, W_A), per_b(past_len * H_A, LANES), per_b(past_len * H_A, LANES),
                  per_b(ts, W_A), per_b(ts, W_A),
                  pl.BlockSpec((H_B, 1, ts, KCAT), lambda bi: (0, bi, 0, 0)),
                  per_b(past_len, KV_LORA), per_b(D_ROPE, past_len), per_b(ts, KCAT)],
        out_specs=(per_b(ts, W_A), per_b(ts, W_B)),
        out_shape=(jax.ShapeDtypeStruct((b, ts, W_A), BF16), jax.ShapeDtypeStruct((b, ts, W_B), BF16)),
        compiler_params=pltpu.CompilerParams(dimension_semantics=("arbitrary",), vmem_limit_bytes=VMEM_LIMIT),
        name="sample_attn",
    )(lam_p, g_sub, w_uvt, q, ck, cv, kn, vn, qcat, cckv, ckpe_t, kcn)


def _out_proj_kernel(final_norm, x_ref, oa_ref, ob_ref, za_ref, zb_ref, ma_ref, mb_ref,
                     w_oa_ref, w_ob_ref, w_out_ref, g_fin_ref, y_ref):
    a = (oa_ref[...].astype(F32) * jax.nn.silu(za_ref[...].astype(F32))).astype(BF16)
    y_a = _dot(a, w_oa_ref[...])
    bb = (ob_ref[...].astype(F32) * jax.nn.silu(zb_ref[...].astype(F32))).astype(BF16)
    y_b = _dot(bb, w_ob_ref[...])
    merged = jax.nn.sigmoid(ma_ref[...].astype(F32)) * y_a + jax.nn.sigmoid(mb_ref[...].astype(F32)) * y_b
    out = x_ref[...] + _dot(merged.astype(BF16), w_out_ref[...])
    y_ref[...] = _rms(out, g_fin_ref[...]) if final_norm else out


def _out_proj(x, oa, ob, za, zb, ma, mb, wts, g_final, final_norm, tm):
    n, d = x.shape
    const = lambda shape: pl.BlockSpec(shape, lambda i: (0,) * len(shape))
    row = lambda w: pl.BlockSpec((tm, w), lambda i: (i, 0))
    return pl.pallas_call(
        functools.partial(_out_proj_kernel, final_norm),
        grid=(n // tm,),
        in_specs=[row(d), row(W_A), row(W_B), row(W_A), row(W_B), row(d), row(d),
                  const((W_A, d)), const((W_B, d)), const((d, d)), const((1, d))],
        out_specs=row(d),
        out_shape=jax.ShapeDtypeStruct((n, d), F32),
        compiler_params=pltpu.CompilerParams(dimension_semantics=("arbitrary",), vmem_limit_bytes=VMEM_LIMIT),
        name="out_proj",
    )(x, oa, ob, za, zb, ma, mb, wts["w_oa"], wts["w_ob"], wts["w_out"], g_final)


def _rope_tables(pos):
    half = DK_A // 2
    inv = ROPE_THETA ** (-jnp.arange(half, dtype=F32) * 2.0 / DK_A)
    ang = pos.astype(F32)[:, None] * inv[None, :]
    cos, sin = jnp.cos(ang), jnp.sin(ang)
    cos = jnp.concatenate([cos, cos, cos, cos], axis=-1)
    sin = jnp.concatenate([-sin, sin, -sin, sin], axis=-1)
    return cos, sin


def _layer_weights(l, w_in, w_uq, w_uk, w_uv, w_oa, w_ob, w_out, norm_in, norm_qa, norm_kva):
    d = w_in.shape[1]
    wi = w_in[l]
    wi = jnp.concatenate([wi[:, :C_KPE + D_ROPE], jnp.zeros((d, LANES - D_ROPE), wi.dtype), wi[:, C_KPE + D_ROPE:]],
                         axis=1)
    uq = w_uq[l]
    uqr = jnp.pad(uq[:, :, D_NOPE:], ((0, 0), (0, 0), (0, LANES - D_ROPE)))
    return {
        "w_in": wi.astype(BF16),
        "w_uqn": uq[:, :, :D_NOPE].reshape(Q_LORA, H_B * D_NOPE).astype(BF16),
        "w_uqr": uqr.reshape(Q_LORA, H_B * LANES).astype(BF16),
        "w_ukt": jnp.transpose(w_uk[l], (1, 2, 0)).astype(BF16),
        "w_ukf": w_uk[l].reshape(KV_LORA, H_B * D_NOPE).astype(BF16),
        "w_uvf": w_uv[l].reshape(KV_LORA, H_B * DV_B).astype(BF16),
        "w_uvt": jnp.transpose(w_uv[l], (1, 0, 2)).astype(BF16),
        "w_oa": w_oa[l].astype(BF16),
        "w_ob": w_ob[l].astype(BF16),
        "w_out": w_out[l].astype(BF16),
        "g_in": norm_in[l][None, :],
        "g_qa": norm_qa[l][None, :],
        "g_kva": norm_kva[l][None, :],
    }


def _row_tile(n, cap):
    tm = min(n, cap)
    assert n % tm == 0
    return tm


def kernel(x_prompt, x_sample, cache_diff_k, cache_diff_v, cache_mla_ckv, cache_mla_kpe, w_in, w_uq, w_uk, w_uv,
           w_oa, w_ob, w_out, lambda_q1, lambda_k1, lambda_q2, lambda_k2, norm_in, norm_qa, norm_kva, norm_subln,
           norm_final):
    b, t, d = x_prompt.shape
    bs, ts, _ = x_sample.shape
    depth = w_in.shape[0]
    past_len = cache_diff_k.shape[2]
    tq = _row_tile(t, 512)
    tm_p = _row_tile(t, 512)
    tm_s = _row_tile(bs * ts, 512)
    assert tm_s % ts == 0 and tq % CHUNK == 0 and tq == tm_p

    cos_p, sin_p = _rope_tables(jnp.arange(t, dtype=jnp.int32))
    cos_s, sin_s = _rope_tables(past_len + jnp.arange(ts, dtype=jnp.int32))
    cos_s = jnp.tile(cos_s, (tm_s // ts, 1))
    sin_s = jnp.tile(sin_s, (tm_s // ts, 1))
    g_final = norm_final[None, :]

    hp = x_prompt.reshape(b * t, d)
    hs = x_sample.reshape(bs * ts, d)
    new_p = ([], [], [], [])
    new_s = ([], [], [], [])
    for l in range(depth):
        lam_init = 0.8 - 0.6 * math.exp(-0.3 * l)
        last = l == depth - 1
        wts = _layer_weights(l, w_in, w_uq, w_uk, w_uv, w_oa, w_ob, w_out, norm_in, norm_qa, norm_kva)
        lam_p = jnp.stack([lambda_q1[l], lambda_k1[l], lambda_q2[l], lambda_k2[l]])
        g_sub = norm_subln[l][None, :]

        (qt, k, v, kb, vt, za, zb, ma, mb, ckv, kpe, kh, qht, vht) = _in_proj(
            hp, cos_p, sin_p, t // tm_p, tm_p, wts, True)
        oa = _diff_attn(qt.reshape(b, t // tq, W_A, tq), kb.reshape(b, t, W_A), vt.reshape(b, t // tq, W_A, tq),
                        lam_p, norm_subln[l][:, None], lam_init, tq)
        ob = _mla_attn(qht.reshape(b, t // tq, H_B, 2 * LANES, tq), kh.reshape(H_B, b, t, 2 * LANES),
                       vht.reshape(b, t // tq, W_B, tq), tq)
        hp = _out_proj(hp, oa.reshape(b * t, W_A), ob.reshape(b * t, W_B), za, zb, ma, mb, wts, g_final, last, tm_p)
        for lst, a in zip(new_p, (k.reshape(b, t, H_A, 2 * DK_A), v.reshape(b, t, H_A, DV_A),
                                  ckv.reshape(b, t, KV_LORA), kpe.reshape(b, t, D_ROPE))):
            lst.append(a)

        (q, k, v, kb, vb, za, zb, ma, mb, ckv, kpe, kcat, qcat) = _in_proj(hs, cos_s, sin_s, 1, tm_s, wts, False)
        oa, ob = _sample_attn(
            q.reshape(bs, ts, W_A), cache_diff_k[l].reshape(bs, past_len * H_A, 2 * DK_A),
            cache_diff_v[l].reshape(bs, past_len * H_A, DV_A), kb.reshape(bs, ts, W_A), vb.reshape(bs, ts, W_A),
            qcat.reshape(H_B, bs, ts, KCAT), cache_mla_ckv[l], jnp.swapaxes(cache_mla_kpe[l], 1, 2),
            kcat.reshape(bs, ts, KCAT), lam_p, g_sub, wts["w_uvt"], lam_init)
        hs = _out_proj(hs, oa.reshape(bs * ts, W_A), ob.reshape(bs * ts, W_B), za, zb, ma, mb, wts, g_final, last, tm_s)
        for lst, a in zip(new_s, (k.reshape(bs, ts, H_A, 2 * DK_A), v.reshape(bs, ts, H_A, DV_A),
                                  ckv.reshape(bs, ts, KV_LORA), kpe.reshape(bs, ts, D_ROPE))):
            lst.append(a)

    return (hp.reshape(b, t, d), hs.reshape(bs, ts, d),
            jnp.stack(new_p[0]), jnp.stack(new_p[1]), jnp.stack(new_p[2]), jnp.stack(new_p[3]),
            jnp.stack(new_s[0]), jnp.stack(new_s[1]), jnp.stack(new_s[2]), jnp.stack(new_s[3]))
```

```python
import functools
import math

import jax
import jax.numpy as jnp
from jax import lax
from jax.experimental import pallas as pl
from jax.experimental.pallas import tpu as pltpu

F32 = jnp.float32
BF16 = jnp.bfloat16

CHUNK = 64
ROPE_THETA = 10000.0
EPS = 1e-6
NEG_INF = -1e30
LOG2E = math.log2(math.e)

H_A = 4
DK_A = 64
DV_A = 2 * DK_A
W_A = H_A * DV_A
H_B = 4
D_NOPE = 128
D_ROPE = 64
DV_B = 128
Q_LORA = 512
KV_LORA = 256
W_B = H_B * DV_B
DIFF_SCALE = DK_A ** -0.5 * LOG2E
MLA_SCALE = (D_NOPE + D_ROPE) ** -0.5 * LOG2E

LANES = 128
KCAT = KV_LORA + LANES
N_SUB = 2
VMEM_LIMIT = 56 * 1024 * 1024

C_QA = 0
C_KA = C_QA + 2 * H_A * DK_A
C_VA = C_KA + 2 * H_A * DK_A
C_ZA = C_VA + W_A
C_QD = C_ZA + W_A
C_CKV = C_QD + Q_LORA
C_KPE = C_CKV + KV_LORA
C_HI = C_KPE + D_ROPE


def _rms(x, g):
    return x * lax.rsqrt(jnp.mean(x * x, axis=-1, keepdims=True) + EPS) * g


def _rope(x, cos, sin_signed):
    w = x.shape[-1]
    lane = lax.broadcasted_iota(jnp.int32, x.shape, 1)
    rot = jnp.where((lane % DK_A) < DK_A // 2, pltpu.roll(x, w - DK_A // 2, 1), pltpu.roll(x, DK_A // 2, 1))
    return x * cos + rot * sin_signed


def _dot(a, b):
    return jnp.dot(a, b, preferred_element_type=F32)


def _dot_nt(a, b):
    return lax.dot_general(a, b, (((1,), (1,)), ((), ())), preferred_element_type=F32)


def _in_proj_kernel(d_model, prompt, x_ref, cos_ref, sin_ref, w_lo_ref, w_hi_ref, g_in_ref, g_qa_ref, g_kva_ref,
                    w_uqn_ref, w_uqr_ref, *refs):
    if prompt:
        (w_ukf_ref, w_uvf_ref, qt_ref, k_ref, v_ref, kb_ref, vt_ref, za_ref, zb_ref, ma_ref, mb_ref, ckv_ref,
         kpe_ref, kh_ref, qht_ref, vht_ref) = refs
    else:
        (w_ukt_ref, q_ref, k_ref, v_ref, kb_ref, vb_ref, za_ref, zb_ref, ma_ref, mb_ref, ckv_ref, kpe_ref,
         kcat_ref, qcat_ref) = refs
    tm = x_ref.shape[0]
    h = _rms(x_ref[...], g_in_ref[...]).astype(BF16)
    cos = cos_ref[...]
    sin = sin_ref[...]

    st = {}

    def lo(c0, n):
        return lambda: _dot(h, w_lo_ref[:, c0:c0 + n])

    def hi(c0, n):
        return lambda: _dot(h, w_hi_ref[:, c0:c0 + n])

    def post_qa(p):
        for j in range(H_A):
            sl = slice(j * LANES, (j + 1) * LANES)
            qr = _rope(p[:, sl], cos, sin) * DIFF_SCALE
            if prompt:
                qt_ref[0, sl, :] = qr.T.astype(BF16)
            else:
                q_ref[:, sl] = qr.astype(BF16)

    def post_ka(p):
        for j in range(H_A):
            sl = slice(j * LANES, (j + 1) * LANES)
            kr = _rope(p[:, sl], cos, sin)
            k_ref[pl.ds(j, tm, stride=H_A), :] = kr
            kb_ref[:, sl] = kr.astype(BF16)

    def post_va(p):
        for j in range(H_A):
            sl = slice(j * LANES, (j + 1) * LANES)
            v_ref[pl.ds(j, tm, stride=H_A), :] = p[:, sl]
            if prompt:
                vt_ref[0, sl, :] = p[:, sl].T.astype(BF16)
        if not prompt:
            vb_ref[...] = p.astype(BF16)

    def post_qd(p):
        st["qd"] = _rms(p, g_qa_ref[...]).astype(BF16)

    def post_q(p):
        st["qn"], st["qr"] = p

    def post_ckv(p):
        ckv = _rms(p, g_kva_ref[...])
        ckv_ref[...] = ckv
        st["ckv_b"] = ckv.astype(BF16)

    def post_kpe(p):
        kp = _rope(jnp.concatenate([p, jnp.zeros_like(p)], axis=1), cos, sin)
        kpe_ref[...] = kp[:, 0:D_ROPE]
        st["kp"] = kp.astype(BF16)

    def post_up(p):
        k_nope, v_up = p
        zeros = jnp.zeros((D_ROPE, tm), BF16)
        for i in range(H_B // 2):
            sl = slice(i * LANES, (i + 1) * LANES)
            pe_t = (_rope(st["qr"][:, sl], cos, sin) * MLA_SCALE).T.astype(BF16)
            for j in (2 * i, 2 * i + 1):
                r0 = (j - 2 * i) * D_ROPE
                qht_ref[0, j, D_NOPE:D_NOPE + D_ROPE, :] = pe_t[r0:r0 + D_ROPE]
                qht_ref[0, j, D_NOPE + D_ROPE:2 * LANES, :] = zeros
        for j in range(H_B):
            sl = slice(j * LANES, (j + 1) * LANES)
            kh_ref[j, :, 0:D_NOPE] = k_nope[:, sl].astype(BF16)
            kh_ref[j, :, D_NOPE:2 * LANES] = st["kp"]
            qht_ref[0, j, 0:D_NOPE, :] = (st["qn"][:, sl] * MLA_SCALE).T.astype(BF16)
            vht_ref[0, sl, :] = v_up[:, sl].T.astype(BF16)

    def post_lat(p):
        for j in range(H_B):
            sl = slice(j * LANES, (j + 1) * LANES)
            qcat_ref[j, :, 0:KV_LORA] = (p[j] * MLA_SCALE).astype(BF16)
            qcat_ref[j, :, KV_LORA:KCAT] = (_rope(st["qr"][:, sl], cos, sin) * MLA_SCALE).astype(BF16)
        kcat_ref[:, 0:KV_LORA] = st["ckv_b"]
        kcat_ref[:, KV_LORA:KCAT] = st["kp"]

    def store_bf16(ref):
        def post(p):
            ref[...] = p.astype(BF16)
        return post

    def q_dots():
        return _dot(st["qd"], w_uqn_ref[...]), _dot(st["qd"], w_uqr_ref[...])

    def up_dots():
        return _dot(st["ckv_b"], w_ukf_ref[...]), _dot(st["ckv_b"], w_uvf_ref[...])

    def lat_dots():
        qn = st["qn"].astype(BF16)
        return [_dot(qn[:, j * LANES:(j + 1) * LANES], w_ukt_ref[j]) for j in range(H_B)]

    stages = [
        (lo(C_QA, 2 * H_A * DK_A), post_qa),
        (lo(C_KA, 2 * H_A * DK_A), post_ka),
        (lo(C_VA, W_A), post_va),
        (lo(C_QD, Q_LORA), post_qd),
        (lo(C_ZA, W_A), store_bf16(za_ref)),
        (q_dots, post_q),
        (lo(C_CKV, KV_LORA), post_ckv),
        (lo(C_KPE, D_ROPE), post_kpe),
        (hi(0, W_B), store_bf16(zb_ref)),
        (up_dots, post_up) if prompt else (lat_dots, post_lat),
        (hi(W_B, d_model), store_bf16(ma_ref)),
        (hi(W_B + d_model, d_model), store_bf16(mb_ref)),
    ]
    cur = stages[0][0]()
    for i, (_, post) in enumerate(stages):
        nxt = stages[i + 1][0]() if i + 1 < len(stages) else None
        post(cur)
        cur = nxt


def _in_proj(x, cos, sin, n_pos_blocks, tm, wts, prompt):
    n, d = x.shape
    n_lo, n_hi = wts["w_lo"].shape[1], wts["w_hi"].shape[1]
    w_uqr = wts["w_uqr_packed"] if prompt else wts["w_uqr"]
    const = lambda shape: pl.BlockSpec(shape, lambda i: (0,) * len(shape))
    row = lambda w: (jax.ShapeDtypeStruct((n, w), BF16), pl.BlockSpec((tm, w), lambda i: (i, 0)))
    row_f32 = lambda w: (jax.ShapeDtypeStruct((n, w), F32), pl.BlockSpec((tm, w), lambda i: (i, 0)))
    heads = (jax.ShapeDtypeStruct((n * H_A, LANES), F32), pl.BlockSpec((tm * H_A, LANES), lambda i: (i, 0)))
    tile_t = (jax.ShapeDtypeStruct((n // tm, W_A, tm), BF16), pl.BlockSpec((1, W_A, tm), lambda i: (i, 0, 0)))
    per_head = lambda w: (jax.ShapeDtypeStruct((H_B, n, w), BF16), pl.BlockSpec((H_B, tm, w), lambda i: (0, i, 0)))
    per_head_t = (jax.ShapeDtypeStruct((n // tm, H_B, 2 * LANES, tm), BF16),
                  pl.BlockSpec((1, H_B, 2 * LANES, tm), lambda i: (i, 0, 0, 0)))
    common = [row(W_A), row(W_B), row(d), row(d), row_f32(KV_LORA), row_f32(D_ROPE)]
    if prompt:
        weights = [wts["w_ukf"], wts["w_uvf"]]
        w_specs = [const((KV_LORA, H_B * D_NOPE)), const((KV_LORA, H_B * DV_B))]
        outs = [tile_t, heads, heads, row(W_A), tile_t] + common + [per_head(2 * LANES), per_head_t, tile_t]
    else:
        weights = [wts["w_ukt"]]
        w_specs = [const((H_B, D_NOPE, KV_LORA))]
        outs = [row(W_A), heads, heads, row(W_A), row(W_A)] + common + [row(KCAT), per_head(KCAT)]
    pos = pl.BlockSpec((tm, LANES), lambda i: (i % n_pos_blocks, 0))
    x_spec = pl.BlockSpec((tm, d), lambda i: (i, 0))
    return pl.pallas_call(
        functools.partial(_in_proj_kernel, d, prompt),
        grid=(n // tm,),
        in_specs=[x_spec, pos, pos, const((d, n_lo)), const((d, n_hi)), const((1, d)), const((1, Q_LORA)),
                  const((1, KV_LORA)), const((Q_LORA, H_B * D_NOPE)), const(w_uqr.shape)] + w_specs,
        out_specs=tuple(o[1] for o in outs),
        out_shape=tuple(o[0] for o in outs),
        compiler_params=pltpu.CompilerParams(dimension_semantics=("arbitrary",), vmem_limit_bytes=VMEM_LIMIT),
        name="in_proj",
    )(x, cos, sin, wts["w_lo"], wts["w_hi"], wts["g_in"], wts["g_qa"], wts["g_kva"], wts["w_uqn"], w_uqr, *weights)


def _lambda(lam_ref, lam_init):
    lp = lam_ref[...]
    a1 = jnp.sum(lp[0:1] * lp[1:2], axis=1, keepdims=True)
    a2 = jnp.sum(lp[2:3] * lp[3:4], axis=1, keepdims=True)
    return jnp.exp(a1) - jnp.exp(a2) + lam_init


def _stack_diff_queries(q):
    lane = lax.broadcasted_iota(jnp.int32, q.shape, 1)
    zero = jnp.zeros_like(q)
    return jnp.concatenate([jnp.where(lane < DK_A, q, zero), jnp.where(lane >= DK_A, q, zero)], axis=0)


def _key_pieces(blk, sub, tq, ts):
    return [(j * tq, (j + 1) * tq, False) for j in range(blk)] + [(blk * tq, blk * tq + (sub + 1) * ts, True)]


def _own_block_mask(rows, cols, sub, ts):
    r = lax.broadcasted_iota(jnp.int32, (rows, cols), 0)
    c = sub * ts + lax.broadcasted_iota(jnp.int32, (rows, cols), 1) % ts
    return (r // CHUNK) <= (c // CHUNK)


def _attention_units(nq, n_heads, n_sub):
    return [(blk, h, sub) for blk in range(nq) for h in range(n_heads) for sub in range(n_sub)]


def _run_units(units, n_pieces, score_piece, vector_work):
    s_next = [score_piece(units[0], j) for j in range(n_pieces(units[0]))]
    for i, unit in enumerate(units):
        s_cur, s_next = s_next, []
        nxt = units[i + 1] if i + 1 < len(units) else None
        n_next = n_pieces(nxt) if nxt is not None else 0
        pending = list(range(n_next))

        def pump(done, total):
            while pending and len(s_next) * total < done * n_next:
                s_next.append(score_piece(nxt, pending.pop(0)))

        vector_work(unit, s_cur, pump)
        pump(1, 1)


def _diff_attn_kernel(lam_init, tq, nq, lam_ref, gsub_ref, qt_ref, k_ref, vt_ref, o_ref):
    lam = _lambda(lam_ref, lam_init)
    ts = tq // N_SUB
    masks = [_own_block_mask((sub + 1) * ts, 2 * ts, sub, ts) for sub in range(N_SUB)]

    def score_piece(unit, j):
        blk, h, sub = unit
        sl = slice(h * LANES, (h + 1) * LANES)
        lo, hi, masked = _key_pieces(blk, sub, tq, ts)[j]
        qt = qt_ref[0, blk, sl, sub * ts:(sub + 1) * ts]
        row = lax.broadcasted_iota(jnp.int32, qt.shape, 0)
        zero = jnp.zeros_like(qt)
        qst = jnp.concatenate([jnp.where(row < DK_A, qt, zero), jnp.where(row >= DK_A, qt, zero)], axis=1)
        sj = _dot(k_ref[0, lo:hi, sl], qst)
        return jnp.where(masks[sub], sj, NEG_INF) if masked else sj

    def vector_work(unit, s, pump):
        blk, h, sub = unit
        sl = slice(h * LANES, (h + 1) * LANES)
        pieces = _key_pieces(blk, sub, tq, ts)
        n, stages = len(pieces), 3 * len(pieces)
        m = None
        for j in range(n):
            mj = jnp.max(s[j], axis=0, keepdims=True)
            m = mj if m is None else jnp.maximum(m, mj)
            pump(j + 1, stages)
        es, l = [], None
        for j in range(n):
            e = jnp.exp2(s[j] - m)
            es.append(e)
            lj = jnp.sum(e, axis=0, keepdims=True)
            l = lj if l is None else l + lj
            pump(n + j + 1, stages)
        l1 = l[:, :ts]
        r = lam * l1 / l[:, ts:]
        o = None
        for j, (lo, hi, _) in enumerate(pieces):
            pj = (es[j][:, :ts] - r * es[j][:, ts:]).astype(BF16)
            oj = _dot(vt_ref[0, lo // tq, sl, 0:hi - lo], pj)
            o = oj if o is None else o + oj
            pump(2 * n + j + 1, stages)
        od = o / l1
        y = od * lax.rsqrt(jnp.mean(od * od, axis=0, keepdims=True) + EPS) * gsub_ref[...] * (1.0 - lam_init)
        q0 = blk * tq + sub * ts
        o_ref[0, q0:q0 + ts, sl] = y.T.astype(BF16)

    _run_units(_attention_units(nq, H_A, N_SUB), lambda u: u[0] + 1, score_piece, vector_work)


def _diff_attn(qt, k, vt, lam_p, g_sub_col, lam_init, tq):
    b, t, _ = k.shape
    nq = t // tq
    return pl.pallas_call(
        functools.partial(_diff_attn_kernel, lam_init, tq, nq),
        grid=(b,),
        in_specs=[pl.BlockSpec((4, DK_A), lambda bi: (0, 0)), pl.BlockSpec((DV_A, 1), lambda bi: (0, 0)),
                  pl.BlockSpec((1, nq, W_A, tq), lambda bi: (bi, 0, 0, 0)),
                  pl.BlockSpec((1, t, W_A), lambda bi: (bi, 0, 0)),
                  pl.BlockSpec((1, nq, W_A, tq), lambda bi: (bi, 0, 0, 0))],
        out_specs=pl.BlockSpec((1, t, W_A), lambda bi: (bi, 0, 0)),
        out_shape=jax.ShapeDtypeStruct((b, t, W_A), BF16),
        compiler_params=pltpu.CompilerParams(dimension_semantics=("arbitrary",), vmem_limit_bytes=VMEM_LIMIT),
        name="diff_attn",
    )(lam_p, g_sub_col, qt, k, vt)


def _mla_attn_kernel(tq, nq, qt_ref, kh_ref, vt_ref, o_ref):
    ts = tq // N_SUB
    masks = [_own_block_mask((sub + 1) * ts, ts, sub, ts) for sub in range(N_SUB)]

    def score_piece(unit, j):
        blk, h, sub = unit
        lo, hi, masked = _key_pieces(blk, sub, tq, ts)[j]
        sj = _dot(kh_ref[h, 0, lo:hi, :], qt_ref[0, blk, h, :, sub * ts:(sub + 1) * ts])
        return jnp.where(masks[sub], sj, NEG_INF) if masked else sj

    def vector_work(unit, s, pump):
        blk, h, sub = unit
        sl = slice(h * DV_B, (h + 1) * DV_B)
        pieces = _key_pieces(blk, sub, tq, ts)
        n, stages = len(pieces), 2 * len(pieces)
        m = None
        for j in range(n):
            mj = jnp.max(s[j], axis=0, keepdims=True)
            m = mj if m is None else jnp.maximum(m, mj)
            pump(j + 1, stages)
        o, l = None, None
        for j, (lo, hi, _) in enumerate(pieces):
            e = jnp.exp2(s[j] - m)
            lj = jnp.sum(e, axis=0, keepdims=True)
            l = lj if l is None else l + lj
            oj = _dot(vt_ref[0, lo // tq, sl, 0:hi - lo], e.astype(BF16))
            o = oj if o is None else o + oj
            pump(n + j + 1, stages)
        q0 = blk * tq + sub * ts
        o_ref[0, q0:q0 + ts, sl] = (o / l).T.astype(BF16)

    _run_units(_attention_units(nq, H_B, N_SUB), lambda u: u[0] + 1, score_piece, vector_work)


def _mla_attn(qht, kh, vht, tq):
    _, b, t, kw = kh.shape
    nq = t // tq
    return pl.pallas_call(
        functools.partial(_mla_attn_kernel, tq, nq),
        grid=(b,),
        in_specs=[pl.BlockSpec((1, nq, H_B, kw, tq), lambda bi: (bi, 0, 0, 0, 0)),
                  pl.BlockSpec((H_B, 1, t, kw), lambda bi: (0, bi, 0, 0)),
                  pl.BlockSpec((1, nq, W_B, tq), lambda bi: (bi, 0, 0, 0))],
        out_specs=pl.BlockSpec((1, t, W_B), lambda bi: (bi, 0, 0)),
        out_shape=jax.ShapeDtypeStruct((b, t, W_B), BF16),
        compiler_params=pltpu.CompilerParams(dimension_semantics=("arbitrary",), vmem_limit_bytes=VMEM_LIMIT),
        name="mla_attn",
    )(qht, kh, vht)


def _two_part_softmax(s_past, s_new, v_past, v_new):
    m = jnp.maximum(jnp.max(s_past, axis=1, keepdims=True), jnp.max(s_new, axis=1, keepdims=True))
    p_past = jnp.exp2(s_past - m)
    p_new = jnp.exp2(s_new - m)
    l = jnp.sum(p_past, axis=1, keepdims=True) + jnp.sum(p_new, axis=1, keepdims=True)
    return (_dot(p_past.astype(BF16), v_past) + _dot(p_new.astype(BF16), v_new)) / l


def _sample_attn_kernel(lam_init, past_len, ts, lam_ref, gsub_ref, w_uvt_ref, q_ref, ck_ref, cv_ref, kn_ref, vn_ref,
                        qcat_ref, cckv_ref, ckpe_t_ref, kcn_ref, oa_ref, ob_ref):
    def new_mask(rows):
        r = past_len + lax.broadcasted_iota(jnp.int32, (rows, ts), 0) % ts
        c = past_len + lax.broadcasted_iota(jnp.int32, (rows, ts), 1)
        return (c // CHUNK) <= (r // CHUNK)

    lam = _lambda(lam_ref, lam_init)
    mask2 = new_mask(2 * ts)
    for h in range(H_A):
        sl = slice(h * LANES, (h + 1) * LANES)
        qs = _stack_diff_queries(q_ref[0, :, sl])
        head_rows = pl.ds(h, past_len, stride=H_A)
        s_past = _dot_nt(qs, ck_ref[0, head_rows, :].astype(BF16))
        s_new = jnp.where(mask2, _dot_nt(qs, kn_ref[0, :, sl]), NEG_INF)
        o = _two_part_softmax(s_past, s_new, cv_ref[0, head_rows, :].astype(BF16), vn_ref[0, :, sl])
        od = o[:ts] - lam * o[ts:]
        oa_ref[0, :, sl] = (_rms(od, gsub_ref[...]) * (1.0 - lam_init)).astype(BF16)

    qs = qcat_ref[:, 0].reshape(H_B * ts, KCAT)
    ckv = cckv_ref[0].astype(BF16)
    kc_new = kcn_ref[0]
    s_past = (_dot_nt(qs[:, 0:KV_LORA], ckv)
              + _dot(qs[:, KV_LORA:KV_LORA + D_ROPE], ckpe_t_ref[0].astype(BF16)))
    s_new = jnp.where(new_mask(H_B * ts), _dot_nt(qs, kc_new), NEG_INF)
    o_lat = _two_part_softmax(s_past, s_new, ckv, kc_new[:, 0:KV_LORA]).astype(BF16)
    for h in range(H_B):
        ob_ref[0, :, h * DV_B:(h + 1) * DV_B] = _dot(o_lat[h * ts:(h + 1) * ts], w_uvt_ref[h]).astype(BF16)


def _sample_attn(q, ck, cv, kn, vn, qcat, cckv, ckpe_t, kcn, lam_p, g_sub, w_uvt, lam_init):
    b, ts, _ = q.shape
    past_len = cckv.shape[1]
    per_b = lambda rows, w: pl.BlockSpec((1, rows, w), lambda bi: (bi, 0, 0))
    const = lambda shape: pl.BlockSpec(shape, lambda bi: (0,) * len(shape))
    return pl.pallas_call(
        functools.partial(_sample_attn_kernel, lam_init, past_len, ts),
        grid=(b,),
        in_specs=[const((4, DK_A)), const((1, DV_A)), const((H_B, KV_LORA, DV_B)),
                  per_b(ts, W_A), per_b(past_len * H_A, LANES), per_b(past_len * H_A, LANES),
                  per_b(ts, W_A), per_b(ts, W_A),
                  pl.BlockSpec((H_B, 1, ts, KCAT), lambda bi: (0, bi, 0, 0)),
                  per_b(past_len, KV_LORA), per_b(D_ROPE, past_len), per_b(ts, KCAT)],
        out_specs=(per_b(ts, W_A), per_b(ts, W_B)),
        out_shape=(jax.ShapeDtypeStruct((b, ts, W_A), BF16), jax.ShapeDtypeStruct((b, ts, W_B), BF16)),
        compiler_params=pltpu.CompilerParams(dimension_semantics=("arbitrary",), vmem_limit_bytes=VMEM_LIMIT),
        name="sample_attn",
    )(lam_p, g_sub, w_uvt, q, ck, cv, kn, vn, qcat, cckv, ckpe_t, kcn)


def _out_proj_kernel(final_norm, x_ref, oa_ref, ob_ref, za_ref, zb_ref, ma_ref, mb_ref,
                     w_oa_ref, w_ob_ref, w_out_ref, g_fin_ref, y_ref):
    a = (oa_ref[...].astype(F32) * jax.nn.silu(za_ref[...].astype(F32))).astype(BF16)
    y_a = _dot(a, w_oa_ref[...])
    bb = (ob_ref[...].astype(F32) * jax.nn.silu(zb_ref[...].astype(F32))).astype(BF16)
    y_b = _dot(bb, w_ob_ref[...])
    merged = jax.nn.sigmoid(ma_ref[...].astype(F32)) * y_a + jax.nn.sigmoid(mb_ref[...].astype(F32)) * y_b
    out = x_ref[...] + _dot(merged.astype(BF16), w_out_ref[...])
    y_ref[...] = _rms(out, g_fin_ref[...]) if final_norm else out


def _out_proj(x, oa, ob, za, zb, ma, mb, wts, g_final, final_norm, tm):
    n, d = x.shape
    const = lambda shape: pl.BlockSpec(shape, lambda i: (0,) * len(shape))
    row = lambda w: pl.BlockSpec((tm, w), lambda i: (i, 0))
    return pl.pallas_call(
        functools.partial(_out_proj_kernel, final_norm),
        grid=(n // tm,),
        in_specs=[row(d), row(W_A), row(W_B), row(W_A), row(W_B), row(d), row(d),
                  const((W_A, d)), const((W_B, d)), const((d, d)), const((1, d))],
        out_specs=row(d),
        out_shape=jax.ShapeDtypeStruct((n, d), F32),
        compiler_params=pltpu.CompilerParams(dimension_semantics=("arbitrary",), vmem_limit_bytes=VMEM_LIMIT),
        name="out_proj",
    )(x, oa, ob, za, zb, ma, mb, wts["w_oa"], wts["w_ob"], wts["w_out"], g_final)


def _rope_tables(pos):
    half = DK_A // 2
    inv = ROPE_THETA ** (-jnp.arange(half, dtype=F32) * 2.0 / DK_A)
    ang = pos.astype(F32)[:, None] * inv[None, :]
    cos, sin = jnp.cos(ang), jnp.sin(ang)
    cos = jnp.concatenate([cos, cos, cos, cos], axis=-1)
    sin = jnp.concatenate([-sin, sin, -sin, sin], axis=-1)
    return cos, sin


def _layer_weights(l, w_in, w_uq, w_uk, w_uv, w_oa, w_ob, w_out, norm_in, norm_qa, norm_kva):
    uq = w_uq[l]
    uqr = jnp.pad(uq[:, :, D_NOPE:], ((0, 0), (0, 0), (0, LANES - D_ROPE)))
    return {
        "w_lo": w_in[l][:, :C_HI].astype(BF16),
        "w_hi": w_in[l][:, C_HI:].astype(BF16),
        "w_uqn": uq[:, :, :D_NOPE].reshape(Q_LORA, H_B * D_NOPE).astype(BF16),
        "w_uqr": uqr.reshape(Q_LORA, H_B * LANES).astype(BF16),
        "w_uqr_packed": uq[:, :, D_NOPE:].reshape(Q_LORA, H_B * D_ROPE).astype(BF16),
        "w_ukt": jnp.transpose(w_uk[l], (1, 2, 0)).astype(BF16),
        "w_ukf": w_uk[l].reshape(KV_LORA, H_B * D_NOPE).astype(BF16),
        "w_uvf": w_uv[l].reshape(KV_LORA, H_B * DV_B).astype(BF16),
        "w_uvt": jnp.transpose(w_uv[l], (1, 0, 2)).astype(BF16),
        "w_oa": w_oa[l].astype(BF16),
        "w_ob": w_ob[l].astype(BF16),
        "w_out": w_out[l].astype(BF16),
        "g_in": norm_in[l][None, :],
        "g_qa": norm_qa[l][None, :],
        "g_kva": norm_kva[l][None, :],
    }


def _row_tile(n, cap):
    tm = min(n, cap)
    assert n % tm == 0
    return tm


def kernel(x_prompt, x_sample, cache_diff_k, cache_diff_v, cache_mla_ckv, cache_mla_kpe, w_in, w_uq, w_uk, w_uv,
           w_oa, w_ob, w_out, lambda_q1, lambda_k1, lambda_q2, lambda_k2, norm_in, norm_qa, norm_kva, norm_subln,
           norm_final):
    b, t, d = x_prompt.shape
    bs, ts, _ = x_sample.shape
    depth = w_in.shape[0]
    past_len = cache_diff_k.shape[2]
    tq = _row_tile(t, 512)
    tm_p = _row_tile(t, 512)
    tm_s = _row_tile(bs * ts, 512)
    assert tm_s % ts == 0 and tq % CHUNK == 0 and tq == tm_p

    cos_p, sin_p = _rope_tables(jnp.arange(t, dtype=jnp.int32))
    cos_s, sin_s = _rope_tables(past_len + jnp.arange(ts, dtype=jnp.int32))
    cos_s = jnp.tile(cos_s, (tm_s // ts, 1))
    sin_s = jnp.tile(sin_s, (tm_s // ts, 1))
    g_final = norm_final[None, :]

    hp = x_prompt.reshape(b * t, d)
    hs = x_sample.reshape(bs * ts, d)
    new_p = ([], [], [], [])
    new_s = ([], [], [], [])
    for l in range(depth):
        lam_init = 0.8 - 0.6 * math.exp(-0.3 * l)
        last = l == depth - 1
        wts = _layer_weights(l, w_in, w_uq, w_uk, w_uv, w_oa, w_ob, w_out, norm_in, norm_qa, norm_kva)
        lam_p = jnp.stack([lambda_q1[l], lambda_k1[l], lambda_q2[l], lambda_k2[l]])
        g_sub = norm_subln[l][None, :]

        (qt, k, v, kb, vt, za, zb, ma, mb, ckv, kpe, kh, qht, vht) = _in_proj(
            hp, cos_p, sin_p, t // tm_p, tm_p, wts, True)
        oa = _diff_attn(qt.reshape(b, t // tq, W_A, tq), kb.reshape(b, t, W_A), vt.reshape(b, t // tq, W_A, tq),
                        lam_p, norm_subln[l][:, None], lam_init, tq)
        ob = _mla_attn(qht.reshape(b, t // tq, H_B, 2 * LANES, tq), kh.reshape(H_B, b, t, 2 * LANES),
                       vht.reshape(b, t // tq, W_B, tq), tq)
        hp = _out_proj(hp, oa.reshape(b * t, W_A), ob.reshape(b * t, W_B), za, zb, ma, mb, wts, g_final, last, tm_p)
        for lst, a in zip(new_p, (k.reshape(b, t, H_A, 2 * DK_A), v.reshape(b, t, H_A, DV_A),
                                  ckv.reshape(b, t, KV_LORA), kpe.reshape(b, t, D_ROPE))):
            lst.append(a)

        (q, k, v, kb, vb, za, zb, ma, mb, ckv, kpe, kcat, qcat) = _in_proj(hs, cos_s, sin_s, 1, tm_s, wts, False)
        oa, ob = _sample_attn(
            q.reshape(bs, ts, W_A), cache_diff_k[l].reshape(bs, past_len * H_A, 2 * DK_A),
            cache_diff_v[l].reshape(bs, past_len * H_A, DV_A), kb.reshape(bs, ts, W_A), vb.reshape(bs, ts, W_A),
            qcat.reshape(H_B, bs, ts, KCAT), cache_mla_ckv[l], jnp.swapaxes(cache_mla_kpe[l], 1, 2),
            kcat.reshape(bs, ts, KCAT), lam_p, g_sub, wts["w_uvt"], lam_init)
        hs = _out_proj(hs, oa.reshape(bs * ts, W_A), ob.reshape(bs * ts, W_B), za, zb, ma, mb, wts, g_final, last, tm_s)
        for lst, a in zip(new_s, (k.reshape(bs, ts, H_A, 2 * DK_A), v.reshape(bs, ts, H_A, DV_A),
                                  ckv.reshape(bs, ts, KV_LORA), kpe.reshape(bs, ts, D_ROPE))):
            lst.append(a)

    return (hp.reshape(b, t, d), hs.reshape(bs, ts, d),
            jnp.stack(new_p[0]), jnp.stack(new_p[1]), jnp.stack(new_p[2]), jnp.stack(new_p[3]),
            jnp.stack(new_s[0]), jnp.stack(new_s[1]), jnp.stack(new_s[2]), jnp.stack(new_s[3]))
```

```python
import functools
import math

import jax
import jax.numpy as jnp
from jax import lax
from jax.experimental import pallas as pl
from jax.experimental.pallas import tpu as pltpu

F32 = jnp.float32
BF16 = jnp.bfloat16

CHUNK = 64
ROPE_THETA = 10000.0
EPS = 1e-6
NEG_INF = -1e30
LOG2E = math.log2(math.e)

H_A = 4
DK_A = 64
DV_A = 2 * DK_A
W_A = H_A * DV_A
H_B = 4
D_NOPE = 128
D_ROPE = 64
DV_B = 128
Q_LORA = 512
KV_LORA = 256
W_B = H_B * DV_B
DIFF_SCALE = DK_A ** -0.5 * LOG2E
MLA_SCALE = (D_NOPE + D_ROPE) ** -0.5 * LOG2E

LANES = 128
KCAT = KV_LORA + LANES
N_SUB = 2
VMEM_LIMIT = 56 * 1024 * 1024

C_QA = 0
C_KA = C_QA + 2 * H_A * DK_A
C_VA = C_KA + 2 * H_A * DK_A
C_ZA = C_VA + W_A
C_QD = C_ZA + W_A
C_CKV = C_QD + Q_LORA
C_KPE = C_CKV + KV_LORA
C_HI = C_KPE + D_ROPE


def _rms(x, g):
    return x * lax.rsqrt(jnp.mean(x * x, axis=-1, keepdims=True) + EPS) * g


def _rope(x, cos, sin_signed):
    w = x.shape[-1]
    lane = lax.broadcasted_iota(jnp.int32, x.shape, 1)
    rot = jnp.where((lane % DK_A) < DK_A // 2, pltpu.roll(x, w - DK_A // 2, 1), pltpu.roll(x, DK_A // 2, 1))
    return x * cos + rot * sin_signed


def _dot(a, b):
    return jnp.dot(a, b, preferred_element_type=F32)


def _dot_nt(a, b):
    return lax.dot_general(a, b, (((1,), (1,)), ((), ())), preferred_element_type=F32)


def _in_proj_kernel(d_model, prompt, x_ref, cos_ref, sin_ref, w_lo_ref, w_hi_ref, g_in_ref, g_qa_ref, g_kva_ref,
                    w_uqn_ref, w_uqr_ref, *refs):
    if prompt:
        (w_ukf_ref, w_uvf_ref, qt_ref, k_ref, v_ref, kb_ref, vt_ref, za_ref, zb_ref, ma_ref, mb_ref, ckv_ref,
         kpe_ref, kh_ref, qht_ref, vht_ref) = refs
    else:
        (w_ukt_ref, q_ref, k_ref, v_ref, kb_ref, vb_ref, za_ref, zb_ref, ma_ref, mb_ref, ckv_ref, kpe_ref,
         kcat_ref, qcat_ref) = refs
    tm = x_ref.shape[0]
    h = _rms(x_ref[...], g_in_ref[...]).astype(BF16)
    cos = cos_ref[...]
    sin = sin_ref[...]

    st = {}

    def lo(c0, n):
        return lambda: _dot(h, w_lo_ref[:, c0:c0 + n])

    def hi(c0, n):
        return lambda: _dot(h, w_hi_ref[:, c0:c0 + n])

    def post_qa(p):
        for j in range(H_A):
            sl = slice(j * LANES, (j + 1) * LANES)
            qr = _rope(p[:, sl], cos, sin) * DIFF_SCALE
            if prompt:
                qt_ref[0, sl, :] = qr.T.astype(BF16)
            else:
                q_ref[:, sl] = qr.astype(BF16)

    def post_ka(p):
        for j in range(H_A):
            sl = slice(j * LANES, (j + 1) * LANES)
            kr = _rope(p[:, sl], cos, sin)
            k_ref[pl.ds(j, tm, stride=H_A), :] = kr
            kb_ref[:, sl] = kr.astype(BF16)

    def post_va(p):
        for j in range(H_A):
            sl = slice(j * LANES, (j + 1) * LANES)
            v_ref[pl.ds(j, tm, stride=H_A), :] = p[:, sl]
            if prompt:
                vt_ref[0, sl, :] = p[:, sl].T.astype(BF16)
        if not prompt:
            vb_ref[...] = p.astype(BF16)

    def post_qd(p):
        st["qd"] = _rms(p, g_qa_ref[...]).astype(BF16)

    def post_q(p):
        st["qn"], st["qr"] = p

    def post_ckv(p):
        ckv = _rms(p, g_kva_ref[...])
        ckv_ref[...] = ckv
        st["ckv_b"] = ckv.astype(BF16)

    def post_kpe(p):
        kp = _rope(jnp.concatenate([p, jnp.zeros_like(p)], axis=1), cos, sin)
        kpe_ref[...] = kp[:, 0:D_ROPE]
        st["kp"] = kp.astype(BF16)

    def post_up(p):
        k_nope, v_up = p
        zeros = jnp.zeros((D_ROPE, tm), BF16)
        for i in range(H_B // 2):
            sl = slice(i * LANES, (i + 1) * LANES)
            pe_t = (_rope(st["qr"][:, sl], cos, sin) * MLA_SCALE).T.astype(BF16)
            for j in (2 * i, 2 * i + 1):
                r0 = (j - 2 * i) * D_ROPE
                qht_ref[0, j, D_NOPE:D_NOPE + D_ROPE, :] = pe_t[r0:r0 + D_ROPE]
                qht_ref[0, j, D_NOPE + D_ROPE:2 * LANES, :] = zeros
        for j in range(H_B):
            sl = slice(j * LANES, (j + 1) * LANES)
            kh_ref[j, :, 0:D_NOPE] = k_nope[:, sl].astype(BF16)
            kh_ref[j, :, D_NOPE:2 * LANES] = st["kp"]
            qht_ref[0, j, 0:D_NOPE, :] = (st["qn"][:, sl] * MLA_SCALE).T.astype(BF16)
            vht_ref[0, sl, :] = v_up[:, sl].T.astype(BF16)

    def post_lat(p):
        for j in range(H_B):
            sl = slice(j * LANES, (j + 1) * LANES)
            qcat_ref[j, :, 0:KV_LORA] = (p[j] * MLA_SCALE).astype(BF16)
            qcat_ref[j, :, KV_LORA:KCAT] = (_rope(st["qr"][:, sl], cos, sin) * MLA_SCALE).astype(BF16)
        kcat_ref[:, 0:KV_LORA] = st["ckv_b"]
        kcat_ref[:, KV_LORA:KCAT] = st["kp"]

    def store_bf16(ref):
        def post(p):
            ref[...] = p.astype(BF16)
        return post

    def q_dots():
        return _dot(st["qd"], w_uqn_ref[...]), _dot(st["qd"], w_uqr_ref[...])

    def up_dots():
        return _dot(st["ckv_b"], w_ukf_ref[...]), _dot(st["ckv_b"], w_uvf_ref[...])

    def lat_dots():
        qn = st["qn"].astype(BF16)
        return [_dot(qn[:, j * LANES:(j + 1) * LANES], w_ukt_ref[j]) for j in range(H_B)]

    stages = [
        (lo(C_QA, 2 * H_A * DK_A), post_qa),
        (lo(C_KA, 2 * H_A * DK_A), post_ka),
        (lo(C_VA, W_A), post_va),
        (lo(C_QD, Q_LORA), post_qd),
        (lo(C_ZA, W_A), store_bf16(za_ref)),
        (q_dots, post_q),
        (lo(C_CKV, KV_LORA), post_ckv),
        (lo(C_KPE, D_ROPE), post_kpe),
        (hi(0, W_B), store_bf16(zb_ref)),
        (up_dots, post_up) if prompt else (lat_dots, post_lat),
        (hi(W_B, d_model), store_bf16(ma_ref)),
        (hi(W_B + d_model, d_model), store_bf16(mb_ref)),
    ]
    cur = stages[0][0]()
    for i, (_, post) in enumerate(stages):
        nxt = stages[i + 1][0]() if i + 1 < len(stages) else None
        post(cur)
        cur = nxt


def _in_proj(x, cos, sin, n_pos_blocks, tm, wts, prompt):
    n, d = x.shape
    n_lo, n_hi = wts["w_lo"].shape[1], wts["w_hi"].shape[1]
    w_uqr = wts["w_uqr_packed"] if prompt else wts["w_uqr"]
    const = lambda shape: pl.BlockSpec(shape, lambda i: (0,) * len(shape))
    row = lambda w: (jax.ShapeDtypeStruct((n, w), BF16), pl.BlockSpec((tm, w), lambda i: (i, 0)))
    row_f32 = lambda w: (jax.ShapeDtypeStruct((n, w), F32), pl.BlockSpec((tm, w), lambda i: (i, 0)))
    heads = (jax.ShapeDtypeStruct((n * H_A, LANES), F32), pl.BlockSpec((tm * H_A, LANES), lambda i: (i, 0)))
    tile_t = (jax.ShapeDtypeStruct((n // tm, W_A, tm), BF16), pl.BlockSpec((1, W_A, tm), lambda i: (i, 0, 0)))
    per_head = lambda w: (jax.ShapeDtypeStruct((H_B, n, w), BF16), pl.BlockSpec((H_B, tm, w), lambda i: (0, i, 0)))
    per_head_t = (jax.ShapeDtypeStruct((n // tm, H_B, 2 * LANES, tm), BF16),
                  pl.BlockSpec((1, H_B, 2 * LANES, tm), lambda i: (i, 0, 0, 0)))
    common = [row(W_A), row(W_B), row(d), row(d), row_f32(KV_LORA), row_f32(D_ROPE)]
    if prompt:
        weights = [wts["w_ukf"], wts["w_uvf"]]
        w_specs = [const((KV_LORA, H_B * D_NOPE)), const((KV_LORA, H_B * DV_B))]
        outs = [tile_t, heads, heads, row(W_A), tile_t] + common + [per_head(2 * LANES), per_head_t, tile_t]
    else:
        weights = [wts["w_ukt"]]
        w_specs = [const((H_B, D_NOPE, KV_LORA))]
        outs = [row(W_A), heads, heads, row(W_A), row(W_A)] + common + [row(KCAT), per_head(KCAT)]
    pos = pl.BlockSpec((tm, LANES), lambda i: (i % n_pos_blocks, 0))
    x_spec = pl.BlockSpec((tm, d), lambda i: (i, 0))
    return pl.pallas_call(
        functools.partial(_in_proj_kernel, d, prompt),
        grid=(n // tm,),
        in_specs=[x_spec, pos, pos, const((d, n_lo)), const((d, n_hi)), const((1, d)), const((1, Q_LORA)),
                  const((1, KV_LORA)), const((Q_LORA, H_B * D_NOPE)), const(w_uqr.shape)] + w_specs,
        out_specs=tuple(o[1] for o in outs),
        out_shape=tuple(o[0] for o in outs),
        compiler_params=pltpu.CompilerParams(dimension_semantics=("arbitrary",), vmem_limit_bytes=VMEM_LIMIT),
        name="in_proj",
    )(x, cos, sin, wts["w_lo"], wts["w_hi"], wts["g_in"], wts["g_qa"], wts["g_kva"], wts["w_uqn"], w_uqr, *weights)


def _lambda(lam_ref, lam_init):
    lp = lam_ref[...]
    a1 = jnp.sum(lp[0:1] * lp[1:2], axis=1, keepdims=True)
    a2 = jnp.sum(lp[2:3] * lp[3:4], axis=1, keepdims=True)
    return jnp.exp(a1) - jnp.exp(a2) + lam_init


def _stack_diff_queries(q):
    lane = lax.broadcasted_iota(jnp.int32, q.shape, 1)
    zero = jnp.zeros_like(q)
    return jnp.concatenate([jnp.where(lane < DK_A, q, zero), jnp.where(lane >= DK_A, q, zero)], axis=0)


def _key_pieces(blk, sub, tq, ts):
    return [(j * tq, (j + 1) * tq, False) for j in range(blk)] + [(blk * tq, blk * tq + (sub + 1) * ts, True)]


def _own_block_mask(rows, cols, sub, ts):
    r = lax.broadcasted_iota(jnp.int32, (rows, cols), 0)
    c = sub * ts + lax.broadcasted_iota(jnp.int32, (rows, cols), 1) % ts
    return (r // CHUNK) <= (c // CHUNK)


def _attention_units(nq, n_heads, n_sub):
    return [(blk, h, sub) for blk in range(nq) for h in range(n_heads) for sub in range(n_sub)]


def _run_units(units, n_pieces, score_piece, vector_work):
    s_next = [score_piece(units[0], j) for j in range(n_pieces(units[0]))]
    for i, unit in enumerate(units):
        s_cur, s_next = s_next, []
        nxt = units[i + 1] if i + 1 < len(units) else None
        n_next = n_pieces(nxt) if nxt is not None else 0
        pending = list(range(n_next))

        def pump(done, total):
            while pending and len(s_next) * total < done * n_next:
                s_next.append(score_piece(nxt, pending.pop(0)))

        vector_work(unit, s_cur, pump)
        pump(1, 1)


def _diff_attn_kernel(lam_init, tq, nq, lam_ref, gsub_ref, qt_ref, k_ref, vt_ref, o_ref):
    lam = _lambda(lam_ref, lam_init)
    ts = tq // N_SUB
    masks = [_own_block_mask((sub + 1) * ts, 2 * ts, sub, ts) for sub in range(N_SUB)]

    def score_piece(unit, j):
        blk, h, sub = unit
        sl = slice(h * LANES, (h + 1) * LANES)
        lo, hi, masked = _key_pieces(blk, sub, tq, ts)[j]
        qt = qt_ref[0, blk, sl, sub * ts:(sub + 1) * ts]
        row = lax.broadcasted_iota(jnp.int32, qt.shape, 0)
        zero = jnp.zeros_like(qt)
        qst = jnp.concatenate([jnp.where(row < DK_A, qt, zero), jnp.where(row >= DK_A, qt, zero)], axis=1)
        sj = _dot(k_ref[0, lo:hi, sl], qst)
        return jnp.where(masks[sub], sj, NEG_INF) if masked else sj

    def vector_work(unit, s, pump):
        blk, h, sub = unit
        sl = slice(h * LANES, (h + 1) * LANES)
        pieces = _key_pieces(blk, sub, tq, ts)
        n, stages = len(pieces), 3 * len(pieces)
        m = None
        for j in range(n):
            mj = jnp.max(s[j], axis=0, keepdims=True)
            m = mj if m is None else jnp.maximum(m, mj)
            pump(j + 1, stages)
        es, l = [], None
        for j in range(n):
            e = jnp.exp2(s[j] - m)
            es.append(e)
            lj = jnp.sum(e, axis=0, keepdims=True)
            l = lj if l is None else l + lj
            pump(n + j + 1, stages)
        l1 = l[:, :ts]
        r = lam * l1 / l[:, ts:]
        o = None
        for j, (lo, hi, _) in enumerate(pieces):
            pj = (es[j][:, :ts] - r * es[j][:, ts:]).astype(BF16)
            oj = _dot(vt_ref[0, lo // tq, sl, 0:hi - lo], pj)
            o = oj if o is None else o + oj
            pump(2 * n + j + 1, stages)
        od = o / l1
        y = od * lax.rsqrt(jnp.mean(od * od, axis=0, keepdims=True) + EPS) * gsub_ref[...] * (1.0 - lam_init)
        q0 = blk * tq + sub * ts
        o_ref[0, q0:q0 + ts, sl] = y.T.astype(BF16)

    _run_units(_attention_units(nq, H_A, N_SUB), lambda u: u[0] + 1, score_piece, vector_work)


def _diff_attn(qt, k, vt, lam_p, g_sub_col, lam_init, tq):
    b, t, _ = k.shape
    nq = t // tq
    return pl.pallas_call(
        functools.partial(_diff_attn_kernel, lam_init, tq, nq),
        grid=(b,),
        in_specs=[pl.BlockSpec((4, DK_A), lambda bi: (0, 0)), pl.BlockSpec((DV_A, 1), lambda bi: (0, 0)),
                  pl.BlockSpec((1, nq, W_A, tq), lambda bi: (bi, 0, 0, 0)),
                  pl.BlockSpec((1, t, W_A), lambda bi: (bi, 0, 0)),
                  pl.BlockSpec((1, nq, W_A, tq), lambda bi: (bi, 0, 0, 0))],
        out_specs=pl.BlockSpec((1, t, W_A), lambda bi: (bi, 0, 0)),
        out_shape=jax.ShapeDtypeStruct((b, t, W_A), BF16),
        compiler_params=pltpu.CompilerParams(dimension_semantics=("arbitrary",), vmem_limit_bytes=VMEM_LIMIT),
        name="diff_attn",
    )(lam_p, g_sub_col, qt, k, vt)


def _mla_attn_kernel(tq, nq, qt_ref, kh_ref, vt_ref, o_ref):
    ts = tq // N_SUB
    masks = [_own_block_mask((sub + 1) * ts, ts, sub, ts) for sub in range(N_SUB)]

    def score_piece(unit, j):
        blk, h, sub = unit
        lo, hi, masked = _key_pieces(blk, sub, tq, ts)[j]
        sj = _dot(kh_ref[h, 0, lo:hi, :], qt_ref[0, blk, h, :, sub * ts:(sub + 1) * ts])
        return jnp.where(masks[sub], sj, NEG_INF) if masked else sj

    def vector_work(unit, s, pump):
        blk, h, sub = unit
        sl = slice(h * DV_B, (h + 1) * DV_B)
        pieces = _key_pieces(blk, sub, tq, ts)
        n, stages = len(pieces), 2 * len(pieces)
        m = None
        for j in range(n):
            mj = jnp.max(s[j], axis=0, keepdims=True)
            m = mj if m is None else jnp.maximum(m, mj)
            pump(j + 1, stages)
        o, l = None, None
        for j, (lo, hi, _) in enumerate(pieces):
            e = jnp.exp2(s[j] - m)
            lj = jnp.sum(e, axis=0, keepdims=True)
            l = lj if l is None else l + lj
            oj = _dot(vt_ref[0, lo // tq, sl, 0:hi - lo], e.astype(BF16))
            o = oj if o is None else o + oj
            pump(n + j + 1, stages)
        q0 = blk * tq + sub * ts
        o_ref[0, q0:q0 + ts, sl] = (o / l).T.astype(BF16)

    _run_units(_attention_units(nq, H_B, N_SUB), lambda u: u[0] + 1, score_piece, vector_work)


def _mla_attn(qht, kh, vht, tq):
    _, b, t, kw = kh.shape
    nq = t // tq
    return pl.pallas_call(
        functools.partial(_mla_attn_kernel, tq, nq),
        grid=(b,),
        in_specs=[pl.BlockSpec((1, nq, H_B, kw, tq), lambda bi: (bi, 0, 0, 0, 0)),
                  pl.BlockSpec((H_B, 1, t, kw), lambda bi: (0, bi, 0, 0)),
                  pl.BlockSpec((1, nq, W_B, tq), lambda bi: (bi, 0, 0, 0))],
        out_specs=pl.BlockSpec((1, t, W_B), lambda bi: (bi, 0, 0)),
        out_shape=jax.ShapeDtypeStruct((b, t, W_B), BF16),
        compiler_params=pltpu.CompilerParams(dimension_semantics=("arbitrary",), vmem_limit_bytes=VMEM_LIMIT),
        name="mla_attn",
    )(qht, kh, vht)


def _two_part_softmax(s_past, s_new, v_past, v_new):
    m = jnp.maximum(jnp.max(s_past, axis=1, keepdims=True), jnp.max(s_new, axis=1, keepdims=True))
    p_past = jnp.exp2(s_past - m)
    p_new = jnp.exp2(s_new - m)
    l = jnp.sum(p_past, axis=1, keepdims=True) + jnp.sum(p_new, axis=1, keepdims=True)
    return (_dot(p_past.astype(BF16), v_past) + _dot(p_new.astype(BF16), v_new)) / l


def _sample_attn_kernel(lam_init, past_len, ts, lam_ref, gsub_ref, w_uvt_ref, q_ref, ck_ref, cv_ref, kn_ref, vn_ref,
                        qcat_ref, cckv_ref, ckpe_t_ref, kcn_ref, oa_ref, ob_ref):
    def new_mask(rows):
        r = past_len + lax.broadcasted_iota(jnp.int32, (rows, ts), 0) % ts
        c = past_len + lax.broadcasted_iota(jnp.int32, (rows, ts), 1)
        return (c // CHUNK) <= (r // CHUNK)

    lam = _lambda(lam_ref, lam_init)
    mask2 = new_mask(2 * ts)
    for h in range(H_A):
        sl = slice(h * LANES, (h + 1) * LANES)
        qs = _stack_diff_queries(q_ref[0, :, sl])
        head_rows = pl.ds(h, past_len, stride=H_A)
        s_past = _dot_nt(qs, ck_ref[0, head_rows, :].astype(BF16))
        s_new = jnp.where(mask2, _dot_nt(qs, kn_ref[0, :, sl]), NEG_INF)
        o = _two_part_softmax(s_past, s_new, cv_ref[0, head_rows, :].astype(BF16), vn_ref[0, :, sl])
        od = o[:ts] - lam * o[ts:]
        oa_ref[0, :, sl] = (_rms(od, gsub_ref[...]) * (1.0 - lam_init)).astype(BF16)

    qs = qcat_ref[:, 0].reshape(H_B * ts, KCAT)
    ckv = cckv_ref[0].astype(BF16)
    kc_new = kcn_ref[0]
    s_past = (_dot_nt(qs[:, 0:KV_LORA], ckv)
              + _dot(qs[:, KV_LORA:KV_LORA + D_ROPE], ckpe_t_ref[0].astype(BF16)))
    s_new = jnp.where(new_mask(H_B * ts), _dot_nt(qs, kc_new), NEG_INF)
    o_lat = _two_part_softmax(s_past, s_new, ckv, kc_new[:, 0:KV_LORA]).astype(BF16)
    for h in range(H_B):
        ob_ref[0, :, h * DV_B:(h + 1) * DV_B] = _dot(o_lat[h * ts:(h + 1) * ts], w_uvt_ref[h]).astype(BF16)


def _sample_attn(q, ck, cv, kn, vn, qcat, cckv, ckpe_t, kcn, lam_p, g_sub, w_uvt, lam_init):
    b, ts, _ = q.shape
    past_len = cckv.shape[1]
    per_b = lambda rows, w: pl.BlockSpec((1, rows, w), lambda bi: (bi, 0, 0))
    const = lambda shape: pl.BlockSpec(shape, lambda bi: (0,) * len(shape))
    return pl.pallas_call(
        functools.partial(_sample_attn_kernel, lam_init, past_len, ts),
        grid=(b,),
        in_specs=[const((4, DK_A)), const((1, DV_A)), const((H_B, KV_LORA, DV_B)),
                  per_b(ts, W_A), per_b(past_len * H_A, LANES), per_b(past_len * H_A, LANES),
                  per_b(ts, W_A), per_b(ts, W_A),
                  pl.BlockSpec((H_B, 1, ts, KCAT), lambda bi: (0, bi, 0, 0)),
                  per_b(past_len, KV_LORA), per_b(D_ROPE, past_len), per_b(ts, KCAT)],
        out_specs=(per_b(ts, W_A), per_b(ts, W_B)),
        out_shape=(jax.ShapeDtypeStruct((b, ts, W_A), BF16), jax.ShapeDtypeStruct((b, ts, W_B), BF16)),
        compiler_params=pltpu.CompilerParams(dimension_semantics=("arbitrary",), vmem_limit_bytes=VMEM_LIMIT),
        name="sample_attn",
    )(lam_p, g_sub, w_uvt, q, ck, cv, kn, vn, qcat, cckv, ckpe_t, kcn)


def _out_proj_kernel(final_norm, x_ref, oa_ref, ob_ref, za_ref, zb_ref, ma_ref, mb_ref,
                     w_oa_ref, w_ob_ref, w_out_ref, g_fin_ref, y_ref):
    a = oa_ref[...] * jax.nn.silu(za_ref[...])
    y_a = _dot(a, w_oa_ref[...])
    bb = ob_ref[...] * jax.nn.silu(zb_ref[...])
    y_b = _dot(bb, w_ob_ref[...])
    merged = jax.nn.sigmoid(ma_ref[...].astype(F32)) * y_a + jax.nn.sigmoid(mb_ref[...].astype(F32)) * y_b
    out = x_ref[...] + _dot(merged.astype(BF16), w_out_ref[...])
    y_ref[...] = _rms(out, g_fin_ref[...]) if final_norm else out


def _out_proj(x, oa, ob, za, zb, ma, mb, wts, g_final, final_norm, tm):
    n, d = x.shape
    const = lambda shape: pl.BlockSpec(shape, lambda i: (0,) * len(shape))
    row = lambda w: pl.BlockSpec((tm, w), lambda i: (i, 0))
    return pl.pallas_call(
        functools.partial(_out_proj_kernel, final_norm),
        grid=(n // tm,),
        in_specs=[row(d), row(W_A), row(W_B), row(W_A), row(W_B), row(d), row(d),
                  const((W_A, d)), const((W_B, d)), const((d, d)), const((1, d))],
        out_specs=row(d),
        out_shape=jax.ShapeDtypeStruct((n, d), F32),
        compiler_params=pltpu.CompilerParams(dimension_semantics=("arbitrary",), vmem_limit_bytes=VMEM_LIMIT),
        name="out_proj",
    )(x, oa, ob, za, zb, ma, mb, wts["w_oa"], wts["w_ob"], wts["w_out"], g_final)


def _rope_tables(pos):
    half = DK_A // 2
    inv = ROPE_THETA ** (-jnp.arange(half, dtype=F32) * 2.0 / DK_A)
    ang = pos.astype(F32)[:, None] * inv[None, :]
    cos, sin = jnp.cos(ang), jnp.sin(ang)
    cos = jnp.concatenate([cos, cos, cos, cos], axis=-1)
    sin = jnp.concatenate([-sin, sin, -sin, sin], axis=-1)
    return cos, sin


def _layer_weights(l, w_in, w_uq, w_uk, w_uv, w_oa, w_ob, w_out, norm_in, norm_qa, norm_kva):
    uq = w_uq[l]
    uqr = jnp.pad(uq[:, :, D_NOPE:], ((0, 0), (0, 0), (0, LANES - D_ROPE)))
    return {
        "w_lo": w_in[l][:, :C_HI].astype(BF16),
        "w_hi": w_in[l][:, C_HI:].astype(BF16),
        "w_uqn": uq[:, :, :D_NOPE].reshape(Q_LORA, H_B * D_NOPE).astype(BF16),
        "w_uqr": uqr.reshape(Q_LORA, H_B * LANES).astype(BF16),
        "w_uqr_packed": uq[:, :, D_NOPE:].reshape(Q_LORA, H_B * D_ROPE).astype(BF16),
        "w_ukt": jnp.transpose(w_uk[l], (1, 2, 0)).astype(BF16),
        "w_ukf": w_uk[l].reshape(KV_LORA, H_B * D_NOPE).astype(BF16),
        "w_uvf": w_uv[l].reshape(KV_LORA, H_B * DV_B).astype(BF16),
        "w_uvt": jnp.transpose(w_uv[l], (1, 0, 2)).astype(BF16),
        "w_oa": w_oa[l].astype(BF16),
        "w_ob": w_ob[l].astype(BF16),
        "w_out": w_out[l].astype(BF16),
        "g_in": norm_in[l][None, :],
        "g_qa": norm_qa[l][None, :],
        "g_kva": norm_kva[l][None, :],
    }


def _row_tile(n, cap):
    tm = min(n, cap)
    assert n % tm == 0
    return tm


def kernel(x_prompt, x_sample, cache_diff_k, cache_diff_v, cache_mla_ckv, cache_mla_kpe, w_in, w_uq, w_uk, w_uv,
           w_oa, w_ob, w_out, lambda_q1, lambda_k1, lambda_q2, lambda_k2, norm_in, norm_qa, norm_kva, norm_subln,
           norm_final):
    b, t, d = x_prompt.shape
    bs, ts, _ = x_sample.shape
    depth = w_in.shape[0]
    past_len = cache_diff_k.shape[2]
    tq = _row_tile(t, 512)
    tm_p = _row_tile(t, 512)
    tm_s = _row_tile(bs * ts, 512)
    assert tm_s % ts == 0 and tq % CHUNK == 0 and tq == tm_p

    cos_p, sin_p = _rope_tables(jnp.arange(t, dtype=jnp.int32))
    cos_s, sin_s = _rope_tables(past_len + jnp.arange(ts, dtype=jnp.int32))
    cos_s = jnp.tile(cos_s, (tm_s // ts, 1))
    sin_s = jnp.tile(sin_s, (tm_s // ts, 1))
    g_final = norm_final[None, :]

    hp = x_prompt.reshape(b * t, d)
    hs = x_sample.reshape(bs * ts, d)
    new_p = ([], [], [], [])
    new_s = ([], [], [], [])
    for l in range(depth):
        lam_init = 0.8 - 0.6 * math.exp(-0.3 * l)
        last = l == depth - 1
        wts = _layer_weights(l, w_in, w_uq, w_uk, w_uv, w_oa, w_ob, w_out, norm_in, norm_qa, norm_kva)
        lam_p = jnp.stack([lambda_q1[l], lambda_k1[l], lambda_q2[l], lambda_k2[l]])
        g_sub = norm_subln[l][None, :]

        (qt, k, v, kb, vt, za, zb, ma, mb, ckv, kpe, kh, qht, vht) = _in_proj(
            hp, cos_p, sin_p, t // tm_p, tm_p, wts, True)
        oa = _diff_attn(qt.reshape(b, t // tq, W_A, tq), kb.reshape(b, t, W_A), vt.reshape(b, t // tq, W_A, tq),
                        lam_p, norm_subln[l][:, None], lam_init, tq)
        ob = _mla_attn(qht.reshape(b, t // tq, H_B, 2 * LANES, tq), kh.reshape(H_B, b, t, 2 * LANES),
                       vht.reshape(b, t // tq, W_B, tq), tq)
        hp = _out_proj(hp, oa.reshape(b * t, W_A), ob.reshape(b * t, W_B), za, zb, ma, mb, wts, g_final, last, tm_p)
        for lst, a in zip(new_p, (k.reshape(b, t, H_A, 2 * DK_A), v.reshape(b, t, H_A, DV_A),
                                  ckv.reshape(b, t, KV_LORA), kpe.reshape(b, t, D_ROPE))):
            lst.append(a)

        (q, k, v, kb, vb, za, zb, ma, mb, ckv, kpe, kcat, qcat) = _in_proj(hs, cos_s, sin_s, 1, tm_s, wts, False)
        oa, ob = _sample_attn(
            q.reshape(bs, ts, W_A), cache_diff_k[l].reshape(bs, past_len * H_A, 2 * DK_A),
            cache_diff_v[l].reshape(bs, past_len * H_A, DV_A), kb.reshape(bs, ts, W_A), vb.reshape(bs, ts, W_A),
            qcat.reshape(H_B, bs, ts, KCAT), cache_mla_ckv[l], jnp.swapaxes(cache_mla_kpe[l], 1, 2),
            kcat.reshape(bs, ts, KCAT), lam_p, g_sub, wts["w_uvt"], lam_init)
        hs = _out_proj(hs, oa.reshape(bs * ts, W_A), ob.reshape(bs * ts, W_B), za, zb, ma, mb, wts, g_final, last, tm_s)
        for lst, a in zip(new_s, (k.reshape(bs, ts, H_A, 2 * DK_A), v.reshape(bs, ts, H_A, DV_A),
                                  ckv.reshape(bs, ts, KV_LORA), kpe.reshape(bs, ts, D_ROPE))):
            lst.append(a)

    return (hp.reshape(b, t, d), hs.reshape(bs, ts, d),
            jnp.stack(new_p[0]), jnp.stack(new_p[1]), jnp.stack(new_p[2]), jnp.stack(new_p[3]),
            jnp.stack(new_s[0]), jnp.stack(new_s[1]), jnp.stack(new_s[2]), jnp.stack(new_s[3]))
```

```python
import functools
import math

import jax
import jax.numpy as jnp
from jax import lax
from jax.experimental import pallas as pl
from jax.experimental.pallas import tpu as pltpu

F32 = jnp.float32
BF16 = jnp.bfloat16

CHUNK = 64
ROPE_THETA = 10000.0
EPS = 1e-6
NEG_INF = -1e30
LOG2E = math.log2(math.e)

H_A = 4
DK_A = 64
DV_A = 2 * DK_A
W_A = H_A * DV_A
H_B = 4
D_NOPE = 128
D_ROPE = 64
DV_B = 128
Q_LORA = 512
KV_LORA = 256
W_B = H_B * DV_B
DIFF_SCALE = DK_A ** -0.5 * LOG2E
MLA_SCALE = (D_NOPE + D_ROPE) ** -0.5 * LOG2E

LANES = 128
KCAT = KV_LORA + LANES
N_SUB = 2
VMEM_LIMIT = 56 * 1024 * 1024

C_QA = 0
C_KA = C_QA + 2 * H_A * DK_A
C_VA = C_KA + 2 * H_A * DK_A
C_ZA = C_VA + W_A
C_QD = C_ZA + W_A
C_CKV = C_QD + Q_LORA
C_KPE = C_CKV + KV_LORA
C_HI = C_KPE + D_ROPE


def _rms(x, g):
    return x * lax.rsqrt(jnp.mean(x * x, axis=-1, keepdims=True) + EPS) * g


def _rope(x, cos, sin_signed):
    w = x.shape[-1]
    lane = lax.broadcasted_iota(jnp.int32, x.shape, 1)
    rot = jnp.where((lane % DK_A) < DK_A // 2, pltpu.roll(x, w - DK_A // 2, 1), pltpu.roll(x, DK_A // 2, 1))
    return x * cos + rot * sin_signed


def _dot(a, b):
    return jnp.dot(a, b, preferred_element_type=F32)


def _dot_nt(a, b):
    return lax.dot_general(a, b, (((1,), (1,)), ((), ())), preferred_element_type=F32)


def _in_proj_kernel(d_model, prompt, x_ref, cos_ref, sin_ref, w_lo_ref, w_hi_ref, g_in_ref, g_qa_ref, g_kva_ref,
                    w_uqn_ref, w_uqr_ref, *refs):
    if prompt:
        (w_ukf_ref, w_uvf_ref, qt_ref, k_ref, v_ref, kb_ref, vt_ref, za_ref, zb_ref, ma_ref, mb_ref, ckv_ref,
         kpe_ref, kh_ref, qht_ref, vht_ref) = refs
    else:
        (w_ukt_ref, q_ref, k_ref, v_ref, kb_ref, vb_ref, za_ref, zb_ref, ma_ref, mb_ref, ckv_ref, kpe_ref,
         kcat_ref, qcat_ref) = refs
    tm = x_ref.shape[0]
    h = _rms(x_ref[...], g_in_ref[...]).astype(BF16)
    cos = cos_ref[...]
    sin = sin_ref[...]

    st = {}

    def lo(c0, n):
        return lambda: _dot(h, w_lo_ref[:, c0:c0 + n])

    def hi(c0, n):
        return lambda: _dot(h, w_hi_ref[:, c0:c0 + n])

    def post_qa(p):
        for j in range(H_A):
            sl = slice(j * LANES, (j + 1) * LANES)
            qr = _rope(p[:, sl], cos, sin) * DIFF_SCALE
            if prompt:
                qt_ref[0, sl, :] = qr.T.astype(BF16)
            else:
                q_ref[:, sl] = qr.astype(BF16)

    def post_ka(p):
        for j in range(H_A):
            sl = slice(j * LANES, (j + 1) * LANES)
            kr = _rope(p[:, sl], cos, sin)
            k_ref[pl.ds(j, tm, stride=H_A), :] = kr
            kb_ref[:, sl] = kr.astype(BF16)

    def post_va(p):
        for j in range(H_A):
            sl = slice(j * LANES, (j + 1) * LANES)
            v_ref[pl.ds(j, tm, stride=H_A), :] = p[:, sl]
            if prompt:
                vt_ref[0, sl, :] = p[:, sl].T.astype(BF16)
        if not prompt:
            vb_ref[...] = p.astype(BF16)

    def post_qd(p):
        st["qd"] = _rms(p, g_qa_ref[...]).astype(BF16)

    def post_q(p):
        st["qn"], st["qr"] = p

    def post_ckv(p):
        ckv = _rms(p, g_kva_ref[...])
        ckv_ref[...] = ckv
        st["ckv_b"] = ckv.astype(BF16)

    def post_kpe(p):
        kp = _rope(jnp.concatenate([p, jnp.zeros_like(p)], axis=1), cos, sin)
        kpe_ref[...] = kp[:, 0:D_ROPE]
        st["kp"] = kp.astype(BF16)

    def post_up(p):
        k_nope, v_up = p
        zeros = jnp.zeros((D_ROPE, tm), BF16)
        for i in range(H_B // 2):
            sl = slice(i * LANES, (i + 1) * LANES)
            pe_t = (_rope(st["qr"][:, sl], cos, sin) * MLA_SCALE).T.astype(BF16)
            for j in (2 * i, 2 * i + 1):
                r0 = (j - 2 * i) * D_ROPE
                qht_ref[0, j, D_NOPE:D_NOPE + D_ROPE, :] = pe_t[r0:r0 + D_ROPE]
                qht_ref[0, j, D_NOPE + D_ROPE:2 * LANES, :] = zeros
        for j in range(H_B):
            sl = slice(j * LANES, (j + 1) * LANES)
            kh_ref[j, :, 0:D_NOPE] = k_nope[:, sl].astype(BF16)
            kh_ref[j, :, D_NOPE:2 * LANES] = st["kp"]
            qht_ref[0, j, 0:D_NOPE, :] = (st["qn"][:, sl] * MLA_SCALE).T.astype(BF16)
            vht_ref[0, sl, :] = v_up[:, sl].T.astype(BF16)

    def post_lat(p):
        for j in range(H_B):
            sl = slice(j * LANES, (j + 1) * LANES)
            qcat_ref[j, :, 0:KV_LORA] = (p[j] * MLA_SCALE).astype(BF16)
            qcat_ref[j, :, KV_LORA:KCAT] = (_rope(st["qr"][:, sl], cos, sin) * MLA_SCALE).astype(BF16)
        kcat_ref[:, 0:KV_LORA] = st["ckv_b"]
        kcat_ref[:, KV_LORA:KCAT] = st["kp"]

    def store_bf16(ref):
        def post(p):
            ref[...] = p.astype(BF16)
        return post

    def q_dots():
        return _dot(st["qd"], w_uqn_ref[...]), _dot(st["qd"], w_uqr_ref[...])

    def up_dots():
        return _dot(st["ckv_b"], w_ukf_ref[...]), _dot(st["ckv_b"], w_uvf_ref[...])

    def lat_dots():
        qn = st["qn"].astype(BF16)
        return [_dot(qn[:, j * LANES:(j + 1) * LANES], w_ukt_ref[j]) for j in range(H_B)]

    stages = [
        (lo(C_QA, 2 * H_A * DK_A), post_qa),
        (lo(C_KA, 2 * H_A * DK_A), post_ka),
        (lo(C_VA, W_A), post_va),
        (lo(C_QD, Q_LORA), post_qd),
        (lo(C_ZA, W_A), store_bf16(za_ref)),
        (q_dots, post_q),
        (lo(C_CKV, KV_LORA), post_ckv),
        (lo(C_KPE, D_ROPE), post_kpe),
        (hi(0, W_B), store_bf16(zb_ref)),
        (up_dots, post_up) if prompt else (lat_dots, post_lat),
        (hi(W_B, d_model), store_bf16(ma_ref)),
        (hi(W_B + d_model, d_model), store_bf16(mb_ref)),
    ]
    cur = stages[0][0]()
    for i, (_, post) in enumerate(stages):
        nxt = stages[i + 1][0]() if i + 1 < len(stages) else None
        post(cur)
        cur = nxt


def _in_proj(x, cos, sin, n_pos_blocks, tm, wts, prompt):
    n, d = x.shape
    n_lo, n_hi = wts["w_lo"].shape[1], wts["w_hi"].shape[1]
    w_uqr = wts["w_uqr_packed"] if prompt else wts["w_uqr"]
    const = lambda shape: pl.BlockSpec(shape, lambda i: (0,) * len(shape))
    row = lambda w: (jax.ShapeDtypeStruct((n, w), BF16), pl.BlockSpec((tm, w), lambda i: (i, 0)))
    row_f32 = lambda w: (jax.ShapeDtypeStruct((n, w), F32), pl.BlockSpec((tm, w), lambda i: (i, 0)))
    heads = (jax.ShapeDtypeStruct((n * H_A, LANES), F32), pl.BlockSpec((tm * H_A, LANES), lambda i: (i, 0)))
    tile_t = (jax.ShapeDtypeStruct((n // tm, W_A, tm), BF16), pl.BlockSpec((1, W_A, tm), lambda i: (i, 0, 0)))
    per_head = lambda w: (jax.ShapeDtypeStruct((H_B, n, w), BF16), pl.BlockSpec((H_B, tm, w), lambda i: (0, i, 0)))
    per_head_t = (jax.ShapeDtypeStruct((n // tm, H_B, 2 * LANES, tm), BF16),
                  pl.BlockSpec((1, H_B, 2 * LANES, tm), lambda i: (i, 0, 0, 0)))
    common = [row(W_A), row(W_B), row(d), row(d), row_f32(KV_LORA), row_f32(D_ROPE)]
    if prompt:
        weights = [wts["w_ukf"], wts["w_uvf"]]
        w_specs = [const((KV_LORA, H_B * D_NOPE)), const((KV_LORA, H_B * DV_B))]
        outs = [tile_t, heads, heads, row(W_A), tile_t] + common + [per_head(2 * LANES), per_head_t, tile_t]
    else:
        weights = [wts["w_ukt"]]
        w_specs = [const((H_B, D_NOPE, KV_LORA))]
        outs = [row(W_A), heads, heads, row(W_A), row(W_A)] + common + [row(KCAT), per_head(KCAT)]
    pos = pl.BlockSpec((tm, LANES), lambda i: (i % n_pos_blocks, 0))
    x_spec = pl.BlockSpec((tm, d), lambda i: (i, 0))
    return pl.pallas_call(
        functools.partial(_in_proj_kernel, d, prompt),
        grid=(n // tm,),
        in_specs=[x_spec, pos, pos, const((d, n_lo)), const((d, n_hi)), const((1, d)), const((1, Q_LORA)),
                  const((1, KV_LORA)), const((Q_LORA, H_B * D_NOPE)), const(w_uqr.shape)] + w_specs,
        out_specs=tuple(o[1] for o in outs),
        out_shape=tuple(o[0] for o in outs),
        compiler_params=pltpu.CompilerParams(dimension_semantics=("arbitrary",), vmem_limit_bytes=VMEM_LIMIT),
        name="in_proj",
    )(x, cos, sin, wts["w_lo"], wts["w_hi"], wts["g_in"], wts["g_qa"], wts["g_kva"], wts["w_uqn"], w_uqr, *weights)


def _lambda(lam_ref, lam_init):
    lp = lam_ref[...]
    a1 = jnp.sum(lp[0:1] * lp[1:2], axis=1, keepdims=True)
    a2 = jnp.sum(lp[2:3] * lp[3:4], axis=1, keepdims=True)
    return jnp.exp(a1) - jnp.exp(a2) + lam_init


def _stack_diff_queries(q):
    lane = lax.broadcasted_iota(jnp.int32, q.shape, 1)
    zero = jnp.zeros_like(q)
    return jnp.concatenate([jnp.where(lane < DK_A, q, zero), jnp.where(lane >= DK_A, q, zero)], axis=0)


def _key_pieces(blk, sub, tq, ts):
    return [(j * tq, (j + 1) * tq, False) for j in range(blk)] + [(blk * tq, blk * tq + (sub + 1) * ts, True)]


def _own_block_mask(rows, cols, sub, ts):
    r = lax.broadcasted_iota(jnp.int32, (rows, cols), 0)
    c = sub * ts + lax.broadcasted_iota(jnp.int32, (rows, cols), 1) % ts
    return (r // CHUNK) <= (c // CHUNK)


def _attention_units(nq, n_heads, n_sub):
    return [(blk, h, sub) for blk in range(nq) for h in range(n_heads) for sub in range(n_sub)]


def _run_units(units, n_pieces, score_piece, vector_work, lead):
    order = [(u, j) for u in units for j in range(n_pieces(u))]
    cum = [0]
    for u in units:
        cum.append(cum[-1] + n_pieces(u))
    cum += [cum[-1]] * (lead + 1)
    scores = {}
    emitted = [0]

    def emit_until(target):
        while emitted[0] < min(target, len(order)):
            u, j = order[emitted[0]]
            scores.setdefault(u, []).append(score_piece(u, j))
            emitted[0] += 1

    emit_until(cum[lead + 1])
    for i, unit in enumerate(units):
        def pump(done, total):
            span = cum[i + lead + 2] - cum[i + lead + 1]
            emit_until(cum[i + lead + 1] + (span * done + total - 1) // total)

        vector_work(unit, scores.pop(unit), pump)
        pump(1, 1)


def _diff_attn_kernel(lam_init, tq, nq, lam_ref, gsub_ref, qt_ref, k_ref, vt_ref, o_ref):
    lam = _lambda(lam_ref, lam_init)
    ts = tq // N_SUB
    masks = [_own_block_mask((sub + 1) * ts, 2 * ts, sub, ts) for sub in range(N_SUB)]

    def score_piece(unit, j):
        blk, h, sub = unit
        sl = slice(h * LANES, (h + 1) * LANES)
        lo, hi, masked = _key_pieces(blk, sub, tq, ts)[j]
        qt = qt_ref[0, blk, sl, sub * ts:(sub + 1) * ts]
        row = lax.broadcasted_iota(jnp.int32, qt.shape, 0)
        zero = jnp.zeros_like(qt)
        qst = jnp.concatenate([jnp.where(row < DK_A, qt, zero), jnp.where(row >= DK_A, qt, zero)], axis=1)
        sj = _dot(k_ref[0, lo:hi, sl], qst)
        return jnp.where(masks[sub], sj, NEG_INF) if masked else sj

    def vector_work(unit, s, pump):
        blk, h, sub = unit
        sl = slice(h * LANES, (h + 1) * LANES)
        pieces = _key_pieces(blk, sub, tq, ts)
        n, stages = len(pieces), 3 * len(pieces)
        m = None
        for j in range(n):
            mj = jnp.max(s[j], axis=0, keepdims=True)
            m = mj if m is None else jnp.maximum(m, mj)
            pump(j + 1, stages)
        es, l = [], None
        for j in range(n):
            e = jnp.exp2(s[j] - m)
            es.append(e)
            lj = jnp.sum(e, axis=0, keepdims=True)
            l = lj if l is None else l + lj
            pump(n + j + 1, stages)
        l1 = l[:, :ts]
        r = lam * l1 / l[:, ts:]
        o = None
        for j, (lo, hi, _) in enumerate(pieces):
            pj = (es[j][:, :ts] - r * es[j][:, ts:]).astype(BF16)
            oj = _dot(vt_ref[0, lo // tq, sl, 0:hi - lo], pj)
            o = oj if o is None else o + oj
            pump(2 * n + j + 1, stages)
        od = o / l1
        y = od * lax.rsqrt(jnp.mean(od * od, axis=0, keepdims=True) + EPS) * gsub_ref[...] * (1.0 - lam_init)
        q0 = blk * tq + sub * ts
        o_ref[0, q0:q0 + ts, sl] = y.T.astype(BF16)

    _run_units(_attention_units(nq, H_A, N_SUB), lambda u: u[0] + 1, score_piece, vector_work, lead=0)


def _diff_attn(qt, k, vt, lam_p, g_sub_col, lam_init, tq):
    b, t, _ = k.shape
    nq = t // tq
    return pl.pallas_call(
        functools.partial(_diff_attn_kernel, lam_init, tq, nq),
        grid=(b,),
        in_specs=[pl.BlockSpec((4, DK_A), lambda bi: (0, 0)), pl.BlockSpec((DV_A, 1), lambda bi: (0, 0)),
                  pl.BlockSpec((1, nq, W_A, tq), lambda bi: (bi, 0, 0, 0)),
                  pl.BlockSpec((1, t, W_A), lambda bi: (bi, 0, 0)),
                  pl.BlockSpec((1, nq, W_A, tq), lambda bi: (bi, 0, 0, 0))],
        out_specs=pl.BlockSpec((1, t, W_A), lambda bi: (bi, 0, 0)),
        out_shape=jax.ShapeDtypeStruct((b, t, W_A), BF16),
        compiler_params=pltpu.CompilerParams(dimension_semantics=("arbitrary",), vmem_limit_bytes=VMEM_LIMIT),
        name="diff_attn",
    )(lam_p, g_sub_col, qt, k, vt)


def _mla_attn_kernel(tq, nq, qt_ref, kh_ref, vt_ref, o_ref):
    ts = tq // N_SUB
    masks = [_own_block_mask((sub + 1) * ts, ts, sub, ts) for sub in range(N_SUB)]

    def score_piece(unit, j):
        blk, h, sub = unit
        lo, hi, masked = _key_pieces(blk, sub, tq, ts)[j]
        sj = _dot(kh_ref[h, 0, lo:hi, :], qt_ref[0, blk, h, :, sub * ts:(sub + 1) * ts])
        return jnp.where(masks[sub], sj, NEG_INF) if masked else sj

    def vector_work(unit, s, pump):
        blk, h, sub = unit
        sl = slice(h * DV_B, (h + 1) * DV_B)
        pieces = _key_pieces(blk, sub, tq, ts)
        n, stages = len(pieces), 2 * len(pieces)
        m = None
        for j in range(n):
            mj = jnp.max(s[j], axis=0, keepdims=True)
            m = mj if m is None else jnp.maximum(m, mj)
            pump(j + 1, stages)
        o, l = None, None
        for j, (lo, hi, _) in enumerate(pieces):
            e = jnp.exp2(s[j] - m)
            lj = jnp.sum(e, axis=0, keepdims=True)
            l = lj if l is None else l + lj
            oj = _dot(vt_ref[0, lo // tq, sl, 0:hi - lo], e.astype(BF16))
            o = oj if o is None else o + oj
            pump(n + j + 1, stages)
        q0 = blk * tq + sub * ts
        o_ref[0, q0:q0 + ts, sl] = (o / l).T.astype(BF16)

    _run_units(_attention_units(nq, H_B, N_SUB), lambda u: u[0] + 1, score_piece, vector_work, lead=1)


def _mla_attn(qht, kh, vht, tq):
    _, b, t, kw = kh.shape
    nq = t // tq
    return pl.pallas_call(
        functools.partial(_mla_attn_kernel, tq, nq),
        grid=(b,),
        in_specs=[pl.BlockSpec((1, nq, H_B, kw, tq), lambda bi: (bi, 0, 0, 0, 0)),
                  pl.BlockSpec((H_B, 1, t, kw), lambda bi: (0, bi, 0, 0)),
                  pl.BlockSpec((1, nq, W_B, tq), lambda bi: (bi, 0, 0, 0))],
        out_specs=pl.BlockSpec((1, t, W_B), lambda bi: (bi, 0, 0)),
        out_shape=jax.ShapeDtypeStruct((b, t, W_B), BF16),
        compiler_params=pltpu.CompilerParams(dimension_semantics=("arbitrary",), vmem_limit_bytes=VMEM_LIMIT),
        name="mla_attn",
    )(qht, kh, vht)


def _two_part_softmax(s_past, s_new, v_past, v_new):
    m = jnp.maximum(jnp.max(s_past, axis=1, keepdims=True), jnp.max(s_new, axis=1, keepdims=True))
    p_past = jnp.exp2(s_past - m)
    p_new = jnp.exp2(s_new - m)
    l = jnp.sum(p_past, axis=1, keepdims=True) + jnp.sum(p_new, axis=1, keepdims=True)
    return (_dot(p_past.astype(BF16), v_past) + _dot(p_new.astype(BF16), v_new)) / l


def _sample_attn_kernel(lam_init, past_len, ts, lam_ref, gsub_ref, w_uvt_ref, q_ref, ck_ref, cv_ref, kn_ref, vn_ref,
                        qcat_ref, cckv_ref, ckpe_t_ref, kcn_ref, oa_ref, ob_ref):
    def new_mask(rows):
        r = past_len + lax.broadcasted_iota(jnp.int32, (rows, ts), 0) % ts
        c = past_len + lax.broadcasted_iota(jnp.int32, (rows, ts), 1)
        return (c // CHUNK) <= (r // CHUNK)

    lam = _lambda(lam_ref, lam_init)
    mask2 = new_mask(2 * ts)

    def diff_scores(h):
        sl = slice(h * LANES, (h + 1) * LANES)
        qs = _stack_diff_queries(q_ref[0, :, sl])
        head_rows = pl.ds(h, past_len, stride=H_A)
        s_past = _dot_nt(qs, ck_ref[0, head_rows, :].astype(BF16))
        s_new = jnp.where(mask2, _dot_nt(qs, kn_ref[0, :, sl]), NEG_INF)
        return s_past, s_new

    def diff_finish(h, s):
        sl = slice(h * LANES, (h + 1) * LANES)
        head_rows = pl.ds(h, past_len, stride=H_A)
        o = _two_part_softmax(s[0], s[1], cv_ref[0, head_rows, :].astype(BF16), vn_ref[0, :, sl])
        od = o[:ts] - lam * o[ts:]
        oa_ref[0, :, sl] = (_rms(od, gsub_ref[...]) * (1.0 - lam_init)).astype(BF16)

    def mla_scores(_):
        qs = qcat_ref[:, 0].reshape(H_B * ts, KCAT)
        s_past = (_dot_nt(qs[:, 0:KV_LORA], cckv_ref[0].astype(BF16))
                  + _dot(qs[:, KV_LORA:KV_LORA + D_ROPE], ckpe_t_ref[0].astype(BF16)))
        s_new = jnp.where(new_mask(H_B * ts), _dot_nt(qs, kcn_ref[0]), NEG_INF)
        return s_past, s_new

    def mla_finish(_, s):
        kc_new = kcn_ref[0]
        o_lat = _two_part_softmax(s[0], s[1], cckv_ref[0].astype(BF16), kc_new[:, 0:KV_LORA]).astype(BF16)
        for h in range(H_B):
            ob_ref[0, :, h * DV_B:(h + 1) * DV_B] = _dot(o_lat[h * ts:(h + 1) * ts], w_uvt_ref[h]).astype(BF16)

    units = [(diff_scores, diff_finish, h) for h in range(H_A)] + [(mla_scores, mla_finish, 0)]
    cur = units[0][0](units[0][2])
    for i, (_, finish, arg) in enumerate(units):
        nxt = units[i + 1][0](units[i + 1][2]) if i + 1 < len(units) else None
        finish(arg, cur)
        cur = nxt


def _sample_attn(q, ck, cv, kn, vn, qcat, cckv, ckpe_t, kcn, lam_p, g_sub, w_uvt, lam_init):
    b, ts, _ = q.shape
    past_len = cckv.shape[1]
    per_b = lambda rows, w: pl.BlockSpec((1, rows, w), lambda bi: (bi, 0, 0))
    const = lambda shape: pl.BlockSpec(shape, lambda bi: (0,) * len(shape))
    return pl.pallas_call(
        functools.partial(_sample_attn_kernel, lam_init, past_len, ts),
        grid=(b,),
        in_specs=[const((4, DK_A)), const((1, DV_A)), const((H_B, KV_LORA, DV_B)),
                  per_b(ts, W_A), per_b(past_len * H_A, LANES), per_b(past_len * H_A, LANES),
                  per_b(ts, W_A), per_b(ts, W_A),
                  pl.BlockSpec((H_B, 1, ts, KCAT), lambda bi: (0, bi, 0, 0)),
                  per_b(past_len, KV_LORA), per_b(D_ROPE, past_len), per_b(ts, KCAT)],
        out_specs=(per_b(ts, W_A), per_b(ts, W_B)),
        out_shape=(jax.ShapeDtypeStruct((b, ts, W_A), BF16), jax.ShapeDtypeStruct((b, ts, W_B), BF16)),
        compiler_params=pltpu.CompilerParams(dimension_semantics=("arbitrary",), vmem_limit_bytes=VMEM_LIMIT),
        name="sample_attn",
    )(lam_p, g_sub, w_uvt, q, ck, cv, kn, vn, qcat, cckv, ckpe_t, kcn)


def _out_proj_kernel(final_norm, x_ref, oa_ref, ob_ref, za_ref, zb_ref, ma_ref, mb_ref,
                     w_oa_ref, w_ob_ref, w_out_ref, g_fin_ref, y_ref):
    a = (oa_ref[...].astype(F32) * jax.nn.silu(za_ref[...].astype(F32))).astype(BF16)
    y_a = _dot(a, w_oa_ref[...])
    bb = (ob_ref[...].astype(F32) * jax.nn.silu(zb_ref[...].astype(F32))).astype(BF16)
    y_b = _dot(bb, w_ob_ref[...])
    merged = jax.nn.sigmoid(ma_ref[...].astype(F32)) * y_a + jax.nn.sigmoid(mb_ref[...].astype(F32)) * y_b
    out = x_ref[...] + _dot(merged.astype(BF16), w_out_ref[...])
    y_ref[...] = _rms(out, g_fin_ref[...]) if final_norm else out


def _out_proj(x, oa, ob, za, zb, ma, mb, wts, g_final, final_norm, tm):
    n, d = x.shape
    const = lambda shape: pl.BlockSpec(shape, lambda i: (0,) * len(shape))
    row = lambda w: pl.BlockSpec((tm, w), lambda i: (i, 0))
    return pl.pallas_call(
        functools.partial(_out_proj_kernel, final_norm),
        grid=(n // tm,),
        in_specs=[row(d), row(W_A), row(W_B), row(W_A), row(W_B), row(d), row(d),
                  const((W_A, d)), const((W_B, d)), const((d, d)), const((1, d))],
        out_specs=row(d),
        out_shape=jax.ShapeDtypeStruct((n, d), F32),
        compiler_params=pltpu.CompilerParams(dimension_semantics=("arbitrary",), vmem_limit_bytes=VMEM_LIMIT),
        name="out_proj",
    )(x, oa, ob, za, zb, ma, mb, wts["w_oa"], wts["w_ob"], wts["w_out"], g_final)


def _rope_tables(pos):
    half = DK_A // 2
    inv = ROPE_THETA ** (-jnp.arange(half, dtype=F32) * 2.0 / DK_A)
    ang = pos.astype(F32)[:, None] * inv[None, :]
    cos, sin = jnp.cos(ang), jnp.sin(ang)
    cos = jnp.concatenate([cos, cos, cos, cos], axis=-1)
    sin = jnp.concatenate([-sin, sin, -sin, sin], axis=-1)
    return cos, sin


def _layer_weights(l, w_in, w_uq, w_uk, w_uv, w_oa, w_ob, w_out, norm_in, norm_qa, norm_kva):
    uq = w_uq[l]
    uqr = jnp.pad(uq[:, :, D_NOPE:], ((0, 0), (0, 0), (0, LANES - D_ROPE)))
    return {
        "w_lo": w_in[l][:, :C_HI].astype(BF16),
        "w_hi": w_in[l][:, C_HI:].astype(BF16),
        "w_uqn": uq[:, :, :D_NOPE].reshape(Q_LORA, H_B * D_NOPE).astype(BF16),
        "w_uqr": uqr.reshape(Q_LORA, H_B * LANES).astype(BF16),
        "w_uqr_packed": uq[:, :, D_NOPE:].reshape(Q_LORA, H_B * D_ROPE).astype(BF16),
        "w_ukt": jnp.transpose(w_uk[l], (1, 2, 0)).astype(BF16),
        "w_ukf": w_uk[l].reshape(KV_LORA, H_B * D_NOPE).astype(BF16),
        "w_uvf": w_uv[l].reshape(KV_LORA, H_B * DV_B).astype(BF16),
        "w_uvt": jnp.transpose(w_uv[l], (1, 0, 2)).astype(BF16),
        "w_oa": w_oa[l].astype(BF16),
        "w_ob": w_ob[l].astype(BF16),
        "w_out": w_out[l].astype(BF16),
        "g_in": norm_in[l][None, :],
        "g_qa": norm_qa[l][None, :],
        "g_kva": norm_kva[l][None, :],
    }


def _row_tile(n, cap):
    tm = min(n, cap)
    assert n % tm == 0
    return tm


def kernel(x_prompt, x_sample, cache_diff_k, cache_diff_v, cache_mla_ckv, cache_mla_kpe, w_in, w_uq, w_uk, w_uv,
           w_oa, w_ob, w_out, lambda_q1, lambda_k1, lambda_q2, lambda_k2, norm_in, norm_qa, norm_kva, norm_subln,
           norm_final):
    b, t, d = x_prompt.shape
    bs, ts, _ = x_sample.shape
    depth = w_in.shape[0]
    past_len = cache_diff_k.shape[2]
    tq = _row_tile(t, 512)
    tm_p = _row_tile(t, 512)
    tm_s = _row_tile(bs * ts, 512)
    assert tm_s % ts == 0 and tq % CHUNK == 0 and tq == tm_p

    cos_p, sin_p = _rope_tables(jnp.arange(t, dtype=jnp.int32))
    cos_s, sin_s = _rope_tables(past_len + jnp.arange(ts, dtype=jnp.int32))
    cos_s = jnp.tile(cos_s, (tm_s // ts, 1))
    sin_s = jnp.tile(sin_s, (tm_s // ts, 1))
    g_final = norm_final[None, :]

    hp = x_prompt.reshape(b * t, d)
    hs = x_sample.reshape(bs * ts, d)
    new_p = ([], [], [], [])
    new_s = ([], [], [], [])
    for l in range(depth):
        lam_init = 0.8 - 0.6 * math.exp(-0.3 * l)
        last = l == depth - 1
        wts = _layer_weights(l, w_in, w_uq, w_uk, w_uv, w_oa, w_ob, w_out, norm_in, norm_qa, norm_kva)
        lam_p = jnp.stack([lambda_q1[l], lambda_k1[l], lambda_q2[l], lambda_k2[l]])
        g_sub = norm_subln[l][None, :]

        (qt, k, v, kb, vt, za, zb, ma, mb, ckv, kpe, kh, qht, vht) = _in_proj(
            hp, cos_p, sin_p, t // tm_p, tm_p, wts, True)
        oa = _diff_attn(qt.reshape(b, t // tq, W_A, tq), kb.reshape(b, t, W_A), vt.reshape(b, t // tq, W_A, tq),
                        lam_p, norm_subln[l][:, None], lam_init, tq)
        ob = _mla_attn(qht.reshape(b, t // tq, H_B, 2 * LANES, tq), kh.reshape(H_B, b, t, 2 * LANES),
                       vht.reshape(b, t // tq, W_B, tq), tq)
        hp = _out_proj(hp, oa.reshape(b * t, W_A), ob.reshape(b * t, W_B), za, zb, ma, mb, wts, g_final, last, tm_p)
        for lst, a in zip(new_p, (k.reshape(b, t, H_A, 2 * DK_A), v.reshape(b, t, H_A, DV_A),
                                  ckv.reshape(b, t, KV_LORA), kpe.reshape(b, t, D_ROPE))):
            lst.append(a)

        (q, k, v, kb, vb, za, zb, ma, mb, ckv, kpe, kcat, qcat) = _in_proj(hs, cos_s, sin_s, 1, tm_s, wts, False)
        oa, ob = _sample_attn(
            q.reshape(bs, ts, W_A), cache_diff_k[l].reshape(bs, past_len * H_A, 2 * DK_A),
            cache_diff_v[l].reshape(bs, past_len * H_A, DV_A), kb.reshape(bs, ts, W_A), vb.reshape(bs, ts, W_A),
            qcat.reshape(H_B, bs, ts, KCAT), cache_mla_ckv[l], jnp.swapaxes(cache_mla_kpe[l], 1, 2),
            kcat.reshape(bs, ts, KCAT), lam_p, g_sub, wts["w_uvt"], lam_init)
        hs = _out_proj(hs, oa.reshape(bs * ts, W_A), ob.reshape(bs * ts, W_B), za, zb, ma, mb, wts, g_final, last, tm_s)
        for lst, a in zip(new_s, (k.reshape(bs, ts, H_A, 2 * DK_A), v.reshape(bs, ts, H_A, DV_A),
                                  ckv.reshape(bs, ts, KV_LORA), kpe.reshape(bs, ts, D_ROPE))):
            lst.append(a)

    return (hp.reshape(b, t, d), hs.reshape(bs, ts, d),
            jnp.stack(new_p[0]), jnp.stack(new_p[1]), jnp.stack(new_p[2]), jnp.stack(new_p[3]),
            jnp.stack(new_s[0]), jnp.stack(new_s[1]), jnp.stack(new_s[2]), jnp.stack(new_s[3]))
```

```python
import functools
import math

import jax
import jax.numpy as jnp
from jax import lax
from jax.experimental import pallas as pl
from jax.experimental.pallas import tpu as pltpu

F32 = jnp.float32
BF16 = jnp.bfloat16

CHUNK = 64
ROPE_THETA = 10000.0
EPS = 1e-6
NEG_INF = -1e30
LOG2E = math.log2(math.e)

H_A = 4
DK_A = 64
DV_A = 2 * DK_A
W_A = H_A * DV_A
H_B = 4
D_NOPE = 128
D_ROPE = 64
DV_B = 128
Q_LORA = 512
KV_LORA = 256
W_B = H_B * DV_B
DIFF_SCALE = DK_A ** -0.5 * LOG2E
MLA_SCALE = (D_NOPE + D_ROPE) ** -0.5 * LOG2E

LANES = 128
KCAT = KV_LORA + LANES
N_SUB = 2
VMEM_LIMIT = 56 * 1024 * 1024

C_QA = 0
C_KA = C_QA + 2 * H_A * DK_A
C_VA = C_KA + 2 * H_A * DK_A
C_ZA = C_VA + W_A
C_QD = C_ZA + W_A
C_CKV = C_QD + Q_LORA
C_KPE = C_CKV + KV_LORA
C_HI = C_KPE + D_ROPE


def _rms(x, g):
    return x * lax.rsqrt(jnp.mean(x * x, axis=-1, keepdims=True) + EPS) * g


def _rope(x, cos, sin_signed):
    w = x.shape[-1]
    lane = lax.broadcasted_iota(jnp.int32, x.shape, 1)
    rot = jnp.where((lane % DK_A) < DK_A // 2, pltpu.roll(x, w - DK_A // 2, 1), pltpu.roll(x, DK_A // 2, 1))
    return x * cos + rot * sin_signed


def _dot(a, b):
    return jnp.dot(a, b, preferred_element_type=F32)


def _dot_nt(a, b):
    return lax.dot_general(a, b, (((1,), (1,)), ((), ())), preferred_element_type=F32)


def _in_proj_kernel(d_model, prompt, x_ref, cos_ref, sin_ref, w_lo_ref, w_hi_ref, g_in_ref, g_qa_ref, g_kva_ref,
                    w_uqn_ref, w_uqr_ref, *refs):
    if prompt:
        (w_ukf_ref, w_uvf_ref, qt_ref, k_ref, v_ref, kb_ref, vt_ref, za_ref, zb_ref, ma_ref, mb_ref, ckv_ref,
         kpe_ref, kh_ref, qht_ref, vht_ref) = refs
    else:
        (w_ukt_ref, q_ref, k_ref, v_ref, kb_ref, vb_ref, za_ref, zb_ref, ma_ref, mb_ref, ckv_ref, kpe_ref,
         kcat_ref, qcat_ref) = refs
    tm = x_ref.shape[0]
    h = _rms(x_ref[...], g_in_ref[...]).astype(BF16)
    cos = cos_ref[...]
    sin = sin_ref[...]

    st = {}

    def lo(c0, n):
        return lambda: _dot(h, w_lo_ref[:, c0:c0 + n])

    def hi(c0, n):
        return lambda: _dot(h, w_hi_ref[:, c0:c0 + n])

    def post_qa(p):
        for j in range(H_A):
            sl = slice(j * LANES, (j + 1) * LANES)
            qr = _rope(p[:, sl], cos, sin) * DIFF_SCALE
            if prompt:
                qt_ref[0, sl, :] = qr.T.astype(BF16)
            else:
                q_ref[:, sl] = qr.astype(BF16)

    def post_ka(p):
        for j in range(H_A):
            sl = slice(j * LANES, (j + 1) * LANES)
            kr = _rope(p[:, sl], cos, sin)
            k_ref[pl.ds(j, tm, stride=H_A), :] = kr
            kb_ref[:, sl] = kr.astype(BF16)

    def post_va(p):
        for j in range(H_A):
            sl = slice(j * LANES, (j + 1) * LANES)
            v_ref[pl.ds(j, tm, stride=H_A), :] = p[:, sl]
            if prompt:
                vt_ref[0, sl, :] = p[:, sl].T.astype(BF16)
        if not prompt:
            vb_ref[...] = p.astype(BF16)

    def post_qd(p):
        st["qd"] = _rms(p, g_qa_ref[...]).astype(BF16)

    def post_q(p):
        st["qn"], st["qr"] = p

    def post_ckv(p):
        ckv = _rms(p, g_kva_ref[...])
        ckv_ref[...] = ckv
        st["ckv_b"] = ckv.astype(BF16)

    def post_kpe(p):
        kp = _rope(jnp.concatenate([p, jnp.zeros_like(p)], axis=1), cos, sin)
        kpe_ref[...] = kp[:, 0:D_ROPE]
        st["kp"] = kp.astype(BF16)

    def post_up(p):
        k_nope, v_up = p
        zeros = jnp.zeros((D_ROPE, tm), BF16)
        for i in range(H_B // 2):
            sl = slice(i * LANES, (i + 1) * LANES)
            pe_t = (_rope(st["qr"][:, sl], cos, sin) * MLA_SCALE).T.astype(BF16)
            for j in (2 * i, 2 * i + 1):
                r0 = (j - 2 * i) * D_ROPE
                qht_ref[0, j, D_NOPE:D_NOPE + D_ROPE, :] = pe_t[r0:r0 + D_ROPE]
                qht_ref[0, j, D_NOPE + D_ROPE:2 * LANES, :] = zeros
        for j in range(H_B):
            sl = slice(j * LANES, (j + 1) * LANES)
            kh_ref[j, :, 0:D_NOPE] = k_nope[:, sl].astype(BF16)
            kh_ref[j, :, D_NOPE:2 * LANES] = st["kp"]
            qht_ref[0, j, 0:D_NOPE, :] = (st["qn"][:, sl] * MLA_SCALE).T.astype(BF16)
            vht_ref[0, sl, :] = v_up[:, sl].T.astype(BF16)

    def post_lat(p):
        for j in range(H_B):
            sl = slice(j * LANES, (j + 1) * LANES)
            qcat_ref[j, :, 0:KV_LORA] = (p[j] * MLA_SCALE).astype(BF16)
            qcat_ref[j, :, KV_LORA:KCAT] = (_rope(st["qr"][:, sl], cos, sin) * MLA_SCALE).astype(BF16)
        kcat_ref[:, 0:KV_LORA] = st["ckv_b"]
        kcat_ref[:, KV_LORA:KCAT] = st["kp"]

    def store_bf16(ref):
        def post(p):
            ref[...] = p.astype(BF16)
        return post

    def q_dots():
        return _dot(st["qd"], w_uqn_ref[...]), _dot(st["qd"], w_uqr_ref[...])

    def up_dots():
        return _dot(st["ckv_b"], w_ukf_ref[...]), _dot(st["ckv_b"], w_uvf_ref[...])

    def lat_dots():
        qn = st["qn"].astype(BF16)
        return [_dot(qn[:, j * LANES:(j + 1) * LANES], w_ukt_ref[j]) for j in range(H_B)]

    stages = [
        (lo(C_QA, 2 * H_A * DK_A), post_qa),
        (lo(C_KA, 2 * H_A * DK_A), post_ka),
        (lo(C_VA, W_A), post_va),
        (lo(C_QD, Q_LORA), post_qd),
        (lo(C_ZA, W_A), store_bf16(za_ref)),
        (q_dots, post_q),
        (lo(C_CKV, KV_LORA), post_ckv),
        (lo(C_KPE, D_ROPE), post_kpe),
        (hi(0, W_B), store_bf16(zb_ref)),
        (up_dots, post_up) if prompt else (lat_dots, post_lat),
        (hi(W_B, d_model), store_bf16(ma_ref)),
        (hi(W_B + d_model, d_model), store_bf16(mb_ref)),
    ]
    cur = stages[0][0]()
    for i, (_, post) in enumerate(stages):
        nxt = stages[i + 1][0]() if i + 1 < len(stages) else None
        post(cur)
        cur = nxt


def _in_proj(x, cos, sin, n_pos_blocks, tm, wts, prompt):
    n, d = x.shape
    n_lo, n_hi = wts["w_lo"].shape[1], wts["w_hi"].shape[1]
    w_uqr = wts["w_uqr_packed"] if prompt else wts["w_uqr"]
    const = lambda shape: pl.BlockSpec(shape, lambda i: (0,) * len(shape))
    row = lambda w: (jax.ShapeDtypeStruct((n, w), BF16), pl.BlockSpec((tm, w), lambda i: (i, 0)))
    row_f32 = lambda w: (jax.ShapeDtypeStruct((n, w), F32), pl.BlockSpec((tm, w), lambda i: (i, 0)))
    heads = (jax.ShapeDtypeStruct((n * H_A, LANES), F32), pl.BlockSpec((tm * H_A, LANES), lambda i: (i, 0)))
    tile_t = (jax.ShapeDtypeStruct((n // tm, W_A, tm), BF16), pl.BlockSpec((1, W_A, tm), lambda i: (i, 0, 0)))
    per_head = lambda w: (jax.ShapeDtypeStruct((H_B, n, w), BF16), pl.BlockSpec((H_B, tm, w), lambda i: (0, i, 0)))
    per_head_t = (jax.ShapeDtypeStruct((n // tm, H_B, 2 * LANES, tm), BF16),
                  pl.BlockSpec((1, H_B, 2 * LANES, tm), lambda i: (i, 0, 0, 0)))
    common = [row(W_A), row(W_B), row(d), row(d), row_f32(KV_LORA), row_f32(D_ROPE)]
    if prompt:
        weights = [wts["w_ukf"], wts["w_uvf"]]
        w_specs = [const((KV_LORA, H_B * D_NOPE)), const((KV_LORA, H_B * DV_B))]
        outs = [tile_t, heads, heads, row(W_A), tile_t] + common + [per_head(2 * LANES), per_head_t, tile_t]
    else:
        weights = [wts["w_ukt"]]
        w_specs = [const((H_B, D_NOPE, KV_LORA))]
        outs = [row(W_A), heads, heads, row(W_A), row(W_A)] + common + [row(KCAT), per_head(KCAT)]
    pos = pl.BlockSpec((tm, LANES), lambda i: (i % n_pos_blocks, 0))
    x_spec = pl.BlockSpec((tm, d), lambda i: (i, 0))
    return pl.pallas_call(
        functools.partial(_in_proj_kernel, d, prompt),
        grid=(n // tm,),
        in_specs=[x_spec, pos, pos, const((d, n_lo)), const((d, n_hi)), const((1, d)), const((1, Q_LORA)),
                  const((1, KV_LORA)), const((Q_LORA, H_B * D_NOPE)), const(w_uqr.shape)] + w_specs,
        out_specs=tuple(o[1] for o in outs),
        out_shape=tuple(o[0] for o in outs),
        compiler_params=pltpu.CompilerParams(dimension_semantics=("arbitrary",), vmem_limit_bytes=VMEM_LIMIT),
        name="in_proj",
    )(x, cos, sin, wts["w_lo"], wts["w_hi"], wts["g_in"], wts["g_qa"], wts["g_kva"], wts["w_uqn"], w_uqr, *weights)


def _lambda(lam_ref, lam_init):
    lp = lam_ref[...]
    a1 = jnp.sum(lp[0:1] * lp[1:2], axis=1, keepdims=True)
    a2 = jnp.sum(lp[2:3] * lp[3:4], axis=1, keepdims=True)
    return jnp.exp(a1) - jnp.exp(a2) + lam_init


def _stack_diff_queries(q):
    lane = lax.broadcasted_iota(jnp.int32, q.shape, 1)
    zero = jnp.zeros_like(q)
    return jnp.concatenate([jnp.where(lane < DK_A, q, zero), jnp.where(lane >= DK_A, q, zero)], axis=0)


def _key_pieces(blk, sub, tq, ts):
    return [(j * tq, (j + 1) * tq, False) for j in range(blk)] + [(blk * tq, blk * tq + (sub + 1) * ts, True)]


def _own_block_mask(rows, cols, sub, ts):
    r = lax.broadcasted_iota(jnp.int32, (rows, cols), 0)
    c = sub * ts + lax.broadcasted_iota(jnp.int32, (rows, cols), 1) % ts
    return (r // CHUNK) <= (c // CHUNK)


def _attention_units(nq, n_heads, n_sub):
    return [(blk, h, sub) for blk in range(nq) for h in range(n_heads) for sub in range(n_sub)]


def _run_units(units, n_pieces, score_piece, vector_work, lead):
    order = [(u, j) for u in units for j in range(n_pieces(u))]
    cum = [0]
    for u in units:
        cum.append(cum[-1] + n_pieces(u))
    cum += [cum[-1]] * (lead + 1)
    scores = {}
    emitted = [0]

    def emit_until(target):
        while emitted[0] < min(target, len(order)):
            u, j = order[emitted[0]]
            scores.setdefault(u, []).append(score_piece(u, j))
            emitted[0] += 1

    emit_until(cum[lead + 1])
    for i, unit in enumerate(units):
        def pump(done, total):
            span = cum[i + lead + 2] - cum[i + lead + 1]
            emit_until(cum[i + lead + 1] + (span * done + total - 1) // total)

        vector_work(unit, scores.pop(unit), pump)
        pump(1, 1)


def _diff_attn_kernel(lam_init, tq, nq, lam_ref, gsub_ref, qt_ref, k_ref, vt_ref, o_ref):
    lam = _lambda(lam_ref, lam_init)
    ts = tq // N_SUB
    masks = [_own_block_mask((sub + 1) * ts, 2 * ts, sub, ts) for sub in range(N_SUB)]

    def score_piece(unit, j):
        blk, h, sub = unit
        sl = slice(h * LANES, (h + 1) * LANES)
        lo, hi, masked = _key_pieces(blk, sub, tq, ts)[j]
        qt = qt_ref[0, blk, sl, sub * ts:(sub + 1) * ts]
        row = lax.broadcasted_iota(jnp.int32, qt.shape, 0)
        zero = jnp.zeros_like(qt)
        qst = jnp.concatenate([jnp.where(row < DK_A, qt, zero), jnp.where(row >= DK_A, qt, zero)], axis=1)
        sj = _dot(k_ref[0, lo:hi, sl], qst)
        return jnp.where(masks[sub], sj, NEG_INF) if masked else sj

    def vector_work(unit, s, pump):
        blk, h, sub = unit
        sl = slice(h * LANES, (h + 1) * LANES)
        pieces = _key_pieces(blk, sub, tq, ts)
        n, stages = len(pieces), 3 * len(pieces)
        m = None
        for j in range(n):
            mj = jnp.max(s[j], axis=0, keepdims=True)
            m = mj if m is None else jnp.maximum(m, mj)
            pump(j + 1, stages)
        es, l = [], None
        for j in range(n):
            e = jnp.exp2(s[j] - m)
            es.append(e)
            lj = jnp.sum(e, axis=0, keepdims=True)
            l = lj if l is None else l + lj
            pump(n + j + 1, stages)
        l1 = l[:, :ts]
        r = lam * l1 / l[:, ts:]
        o = None
        for j, (lo, hi, _) in enumerate(pieces):
            pj = (es[j][:, :ts] - r * es[j][:, ts:]).astype(BF16)
            oj = _dot(vt_ref[0, lo // tq, sl, 0:hi - lo], pj)
            o = oj if o is None else o + oj
            pump(2 * n + j + 1, stages)
        od = o / l1
        y = od * lax.rsqrt(jnp.mean(od * od, axis=0, keepdims=True) + EPS) * gsub_ref[...] * (1.0 - lam_init)
        q0 = blk * tq + sub * ts
        o_ref[0, q0:q0 + ts, sl] = y.T.astype(BF16)

    _run_units(_attention_units(nq, H_A, N_SUB), lambda u: u[0] + 1, score_piece, vector_work, lead=0)


def _diff_attn(qt, k, vt, lam_p, g_sub_col, lam_init, tq):
    b, t, _ = k.shape
    nq = t // tq
    return pl.pallas_call(
        functools.partial(_diff_attn_kernel, lam_init, tq, nq),
        grid=(b,),
        in_specs=[pl.BlockSpec((4, DK_A), lambda bi: (0, 0)), pl.BlockSpec((DV_A, 1), lambda bi: (0, 0)),
                  pl.BlockSpec((1, nq, W_A, tq), lambda bi: (bi, 0, 0, 0)),
                  pl.BlockSpec((1, t, W_A), lambda bi: (bi, 0, 0)),
                  pl.BlockSpec((1, nq, W_A, tq), lambda bi: (bi, 0, 0, 0))],
        out_specs=pl.BlockSpec((1, t, W_A), lambda bi: (bi, 0, 0)),
        out_shape=jax.ShapeDtypeStruct((b, t, W_A), BF16),
        compiler_params=pltpu.CompilerParams(dimension_semantics=("arbitrary",), vmem_limit_bytes=VMEM_LIMIT),
        name="diff_attn",
    )(lam_p, g_sub_col, qt, k, vt)


def _mla_attn_kernel(tq, nq, qt_ref, kh_ref, vt_ref, o_ref):
    ts = tq // N_SUB
    masks = [_own_block_mask((sub + 1) * ts, ts, sub, ts) for sub in range(N_SUB)]

    def score_piece(unit, j):
        blk, h, sub = unit
        lo, hi, masked = _key_pieces(blk, sub, tq, ts)[j]
        sj = _dot(kh_ref[h, 0, lo:hi, :], qt_ref[0, blk, h, :, sub * ts:(sub + 1) * ts])
        return jnp.where(masks[sub], sj, NEG_INF) if masked else sj

    def vector_work(unit, s, pump):
        blk, h, sub = unit
        sl = slice(h * DV_B, (h + 1) * DV_B)
        pieces = _key_pieces(blk, sub, tq, ts)
        n, stages = len(pieces), 2 * len(pieces)
        m = None
        for j in range(n):
            mj = jnp.max(s[j], axis=0, keepdims=True)
            m = mj if m is None else jnp.maximum(m, mj)
            pump(j + 1, stages)
        o, l = None, None
        for j, (lo, hi, _) in enumerate(pieces):
            e = jnp.exp2(s[j] - m)
            lj = jnp.sum(e, axis=0, keepdims=True)
            l = lj if l is None else l + lj
            oj = _dot(vt_ref[0, lo // tq, sl, 0:hi - lo], e.astype(BF16))
            o = oj if o is None else o + oj
            pump(n + j + 1, stages)
        q0 = blk * tq + sub * ts
        o_ref[0, q0:q0 + ts, sl] = (o / l).T.astype(BF16)

    _run_units(_attention_units(nq, H_B, N_SUB), lambda u: u[0] + 1, score_piece, vector_work, lead=2)


def _mla_attn(qht, kh, vht, tq):
    _, b, t, kw = kh.shape
    nq = t // tq
    return pl.pallas_call(
        functools.partial(_mla_attn_kernel, tq, nq),
        grid=(b,),
        in_specs=[pl.BlockSpec((1, nq, H_B, kw, tq), lambda bi: (bi, 0, 0, 0, 0)),
                  pl.BlockSpec((H_B, 1, t, kw), lambda bi: (0, bi, 0, 0)),
                  pl.BlockSpec((1, nq, W_B, tq), lambda bi: (bi, 0, 0, 0))],
        out_specs=pl.BlockSpec((1, t, W_B), lambda bi: (bi, 0, 0)),
        out_shape=jax.ShapeDtypeStruct((b, t, W_B), BF16),
        compiler_params=pltpu.CompilerParams(dimension_semantics=("arbitrary",), vmem_limit_bytes=VMEM_LIMIT),
        name="mla_attn",
    )(qht, kh, vht)


def _two_part_softmax(s_past, s_new, v_past, v_new):
    m = jnp.maximum(jnp.max(s_past, axis=1, keepdims=True), jnp.max(s_new, axis=1, keepdims=True))
    p_past = jnp.exp2(s_past - m)
    p_new = jnp.exp2(s_new - m)
    l = jnp.sum(p_past, axis=1, keepdims=True) + jnp.sum(p_new, axis=1, keepdims=True)
    return (_dot(p_past.astype(BF16), v_past) + _dot(p_new.astype(BF16), v_new)) / l


def _sample_attn_kernel(lam_init, past_len, ts, lam_ref, gsub_ref, w_uvt_ref, q_ref, ck_ref, cv_ref, kn_ref, vn_ref,
                        qcat_ref, cckv_ref, ckpe_t_ref, kcn_ref, oa_ref, ob_ref):
    def new_mask(rows):
        r = past_len + lax.broadcasted_iota(jnp.int32, (rows, ts), 0) % ts
        c = past_len + lax.broadcasted_iota(jnp.int32, (rows, ts), 1)
        return (c // CHUNK) <= (r // CHUNK)

    lam = _lambda(lam_ref, lam_init)
    mask2 = new_mask(2 * ts)

    def diff_scores(h):
        sl = slice(h * LANES, (h + 1) * LANES)
        qs = _stack_diff_queries(q_ref[0, :, sl])
        head_rows = pl.ds(h, past_len, stride=H_A)
        s_past = _dot_nt(qs, ck_ref[0, head_rows, :].astype(BF16))
        s_new = jnp.where(mask2, _dot_nt(qs, kn_ref[0, :, sl]), NEG_INF)
        return s_past, s_new

    def diff_finish(h, s):
        sl = slice(h * LANES, (h + 1) * LANES)
        head_rows = pl.ds(h, past_len, stride=H_A)
        o = _two_part_softmax(s[0], s[1], cv_ref[0, head_rows, :].astype(BF16), vn_ref[0, :, sl])
        od = o[:ts] - lam * o[ts:]
        oa_ref[0, :, sl] = (_rms(od, gsub_ref[...]) * (1.0 - lam_init)).astype(BF16)

    def mla_scores(_):
        qs = qcat_ref[:, 0].reshape(H_B * ts, KCAT)
        s_past = (_dot_nt(qs[:, 0:KV_LORA], cckv_ref[0].astype(BF16))
                  + _dot(qs[:, KV_LORA:KV_LORA + D_ROPE], ckpe_t_ref[0].astype(BF16)))
        s_new = jnp.where(new_mask(H_B * ts), _dot_nt(qs, kcn_ref[0]), NEG_INF)
        return s_past, s_new

    def mla_finish(_, s):
        kc_new = kcn_ref[0]
        o_lat = _two_part_softmax(s[0], s[1], cckv_ref[0].astype(BF16), kc_new[:, 0:KV_LORA]).astype(BF16)
        for h in range(H_B):
            ob_ref[0, :, h * DV_B:(h + 1) * DV_B] = _dot(o_lat[h * ts:(h + 1) * ts], w_uvt_ref[h]).astype(BF16)

    units = [(diff_scores, diff_finish, h) for h in range(H_A)] + [(mla_scores, mla_finish, 0)]
    cur = units[0][0](units[0][2])
    for i, (_, finish, arg) in enumerate(units):
        nxt = units[i + 1][0](units[i + 1][2]) if i + 1 < len(units) else None
        finish(arg, cur)
        cur = nxt


def _sample_attn(q, ck, cv, kn, vn, qcat, cckv, ckpe_t, kcn, lam_p, g_sub, w_uvt, lam_init):
    b, ts, _ = q.shape
    past_len = cckv.shape[1]
    per_b = lambda rows, w: pl.BlockSpec((1, rows, w), lambda bi: (bi, 0, 0))
    const = lambda shape: pl.BlockSpec(shape, lambda bi: (0,) * len(shape))
    return pl.pallas_call(
        functools.partial(_sample_attn_kernel, lam_init, past_len, ts),
        grid=(b,),
        in_specs=[const((4, DK_A)), const((1, DV_A)), const((H_B, KV_LORA, DV_B)),
                  per_b(ts, W_A), per_b(past_len * H_A, LANES), per_b(past_len * H_A, LANES),
                  per_b(ts, W_A), per_b(ts, W_A),
                  pl.BlockSpec((H_B, 1, ts, KCAT), lambda bi: (0, bi, 0, 0)),
                  per_b(past_len, KV_LORA), per_b(D_ROPE, past_len), per_b(ts, KCAT)],
        out_specs=(per_b(ts, W_A), per_b(ts, W_B)),
        out_shape=(jax.ShapeDtypeStruct((b, ts, W_A), BF16), jax.ShapeDtypeStruct((b, ts, W_B), BF16)),
        compiler_params=pltpu.CompilerParams(dimension_semantics=("arbitrary",), vmem_limit_bytes=VMEM_LIMIT),
        name="sample_attn",
    )(lam_p, g_sub, w_uvt, q, ck, cv, kn, vn, qcat, cckv, ckpe_t, kcn)


def _out_proj_kernel(final_norm, x_ref, oa_ref, ob_ref, za_ref, zb_ref, ma_ref, mb_ref,
                     w_oa_ref, w_ob_ref, w_out_ref, g_fin_ref, y_ref):
    a = (oa_ref[...].astype(F32) * jax.nn.silu(za_ref[...].astype(F32))).astype(BF16)
    y_a = _dot(a, w_oa_ref[...])
    bb = (ob_ref[...].astype(F32) * jax.nn.silu(zb_ref[...].astype(F32))).astype(BF16)
    y_b = _dot(bb, w_ob_ref[...])
    merged = jax.nn.sigmoid(ma_ref[...].astype(F32)) * y_a + jax.nn.sigmoid(mb_ref[...].astype(F32)) * y_b
    out = x_ref[...] + _dot(merged.astype(BF16), w_out_ref[...])
    y_ref[...] = _rms(out, g_fin_ref[...]) if final_norm else out


def _out_proj(x, oa, ob, za, zb, ma, mb, wts, g_final, final_norm, tm):
    n, d = x.shape
    const = lambda shape: pl.BlockSpec(shape, lambda i: (0,) * len(shape))
    row = lambda w: pl.BlockSpec((tm, w), lambda i: (i, 0))
    return pl.pallas_call(
        functools.partial(_out_proj_kernel, final_norm),
        grid=(n // tm,),
        in_specs=[row(d), row(W_A), row(W_B), row(W_A), row(W_B), row(d), row(d),
                  const((W_A, d)), const((W_B, d)), const((d, d)), const((1, d))],
        out_specs=row(d),
        out_shape=jax.ShapeDtypeStruct((n, d), F32),
        compiler_params=pltpu.CompilerParams(dimension_semantics=("arbitrary",), vmem_limit_bytes=VMEM_LIMIT),
        name="out_proj",
    )(x, oa, ob, za, zb, ma, mb, wts["w_oa"], wts["w_ob"], wts["w_out"], g_final)


def _rope_tables(pos):
    half = DK_A // 2
    inv = ROPE_THETA ** (-jnp.arange(half, dtype=F32) * 2.0 / DK_A)
    ang = pos.astype(F32)[:, None] * inv[None, :]
    cos, sin = jnp.cos(ang), jnp.sin(ang)
    cos = jnp.concatenate([cos, cos, cos, cos], axis=-1)
    sin = jnp.concatenate([-sin, sin, -sin, sin], axis=-1)
    return cos, sin


def _layer_weights(l, w_in, w_uq, w_uk, w_uv, w_oa, w_ob, w_out, norm_in, norm_qa, norm_kva):
    uq = w_uq[l]
    uqr = jnp.pad(uq[:, :, D_NOPE:], ((0, 0), (0, 0), (0, LANES - D_ROPE)))
    return {
        "w_lo": w_in[l][:, :C_HI].astype(BF16),
        "w_hi": w_in[l][:, C_HI:].astype(BF16),
        "w_uqn": uq[:, :, :D_NOPE].reshape(Q_LORA, H_B * D_NOPE).astype(BF16),
        "w_uqr": uqr.reshape(Q_LORA, H_B * LANES).astype(BF16),
        "w_uqr_packed": uq[:, :, D_NOPE:].reshape(Q_LORA, H_B * D_ROPE).astype(BF16),
        "w_ukt": jnp.transpose(w_uk[l], (1, 2, 0)).astype(BF16),
        "w_ukf": w_uk[l].reshape(KV_LORA, H_B * D_NOPE).astype(BF16),
        "w_uvf": w_uv[l].reshape(KV_LORA, H_B * DV_B).astype(BF16),
        "w_uvt": jnp.transpose(w_uv[l], (1, 0, 2)).astype(BF16),
        "w_oa": w_oa[l].astype(BF16),
        "w_ob": w_ob[l].astype(BF16),
        "w_out": w_out[l].astype(BF16),
        "g_in": norm_in[l][None, :],
        "g_qa": norm_qa[l][None, :],
        "g_kva": norm_kva[l][None, :],
    }


def _row_tile(n, cap):
    tm = min(n, cap)
    assert n % tm == 0
    return tm


def kernel(x_prompt, x_sample, cache_diff_k, cache_diff_v, cache_mla_ckv, cache_mla_kpe, w_in, w_uq, w_uk, w_uv,
           w_oa, w_ob, w_out, lambda_q1, lambda_k1, lambda_q2, lambda_k2, norm_in, norm_qa, norm_kva, norm_subln,
           norm_final):
    b, t, d = x_prompt.shape
    bs, ts, _ = x_sample.shape
    depth = w_in.shape[0]
    past_len = cache_diff_k.shape[2]
    tq = _row_tile(t, 512)
    tm_p = _row_tile(t, 512)
    tm_s = _row_tile(bs * ts, 512)
    assert tm_s % ts == 0 and tq % CHUNK == 0 and tq == tm_p

    cos_p, sin_p = _rope_tables(jnp.arange(t, dtype=jnp.int32))
    cos_s, sin_s = _rope_tables(past_len + jnp.arange(ts, dtype=jnp.int32))
    cos_s = jnp.tile(cos_s, (tm_s // ts, 1))
    sin_s = jnp.tile(sin_s, (tm_s // ts, 1))
    g_final = norm_final[None, :]

    hp = x_prompt.reshape(b * t, d)
    hs = x_sample.reshape(bs * ts, d)
    new_p = ([], [], [], [])
    new_s = ([], [], [], [])
    for l in range(depth):
        lam_init = 0.8 - 0.6 * math.exp(-0.3 * l)
        last = l == depth - 1
        wts = _layer_weights(l, w_in, w_uq, w_uk, w_uv, w_oa, w_ob, w_out, norm_in, norm_qa, norm_kva)
        lam_p = jnp.stack([lambda_q1[l], lambda_k1[l], lambda_q2[l], lambda_k2[l]])
        g_sub = norm_subln[l][None, :]

        (qt, k, v, kb, vt, za, zb, ma, mb, ckv, kpe, kh, qht, vht) = _in_proj(
            hp, cos_p, sin_p, t // tm_p, tm_p, wts, True)
        oa = _diff_attn(qt.reshape(b, t // tq, W_A, tq), kb.reshape(b, t, W_A), vt.reshape(b, t // tq, W_A, tq),
                        lam_p, norm_subln[l][:, None], lam_init, tq)
        ob = _mla_attn(qht.reshape(b, t // tq, H_B, 2 * LANES, tq), kh.reshape(H_B, b, t, 2 * LANES),
                       vht.reshape(b, t // tq, W_B, tq), tq)
        hp = _out_proj(hp, oa.reshape(b * t, W_A), ob.reshape(b * t, W_B), za, zb, ma, mb, wts, g_final, last, tm_p)
        for lst, a in zip(new_p, (k.reshape(b, t, H_A, 2 * DK_A), v.reshape(b, t, H_A, DV_A),
                                  ckv.reshape(b, t, KV_LORA), kpe.reshape(b, t, D_ROPE))):
            lst.append(a)

        (q, k, v, kb, vb, za, zb, ma, mb, ckv, kpe, kcat, qcat) = _in_proj(hs, cos_s, sin_s, 1, tm_s, wts, False)
        oa, ob = _sample_attn(
            q.reshape(bs, ts, W_A), cache_diff_k[l].reshape(bs, past_len * H_A, 2 * DK_A),
            cache_diff_v[l].reshape(bs, past_len * H_A, DV_A), kb.reshape(bs, ts, W_A), vb.reshape(bs, ts, W_A),
            qcat.reshape(H_B, bs, ts, KCAT), cache_mla_ckv[l], jnp.swapaxes(cache_mla_kpe[l], 1, 2),
            kcat.reshape(bs, ts, KCAT), lam_p, g_sub, wts["w_uvt"], lam_init)
        hs = _out_proj(hs, oa.reshape(bs * ts, W_A), ob.reshape(bs * ts, W_B), za, zb, ma, mb, wts, g_final, last, tm_s)
        for lst, a in zip(new_s, (k.reshape(bs, ts, H_A, 2 * DK_A), v.reshape(bs, ts, H_A, DV_A),
                                  ckv.reshape(bs, ts, KV_LORA), kpe.reshape(bs, ts, D_ROPE))):
            lst.append(a)

    return (hp.reshape(b, t, d), hs.reshape(bs, ts, d),
            jnp.stack(new_p[0]), jnp.stack(new_p[1]), jnp.stack(new_p[2]), jnp.stack(new_p[3]),
            jnp.stack(new_s[0]), jnp.stack(new_s[1]), jnp.stack(new_s[2]), jnp.stack(new_s[3]))
```
